```python
import jax, jax.numpy as jnp
from jax import lax
import numpy as np

D_MODEL = 2048
BATCH = 16
SEQ = 2048
DEPTH = 2

GRID_W = 64
CTX_LEN = 256
N_HEADS = 16
N_KV_HEADS = 4
HEAD_DIM = D_MODEL // N_HEADS
GROUP = N_HEADS // N_KV_HEADS
Q_BLOCK = 128
ROPE_THETA = 10000.0
ATTN_SCALE = HEAD_DIM ** -0.5
CONV_WIDTH = D_MODEL
CONV_K = 3
D_FF = 7 * D_MODEL // 2
N_EXPERTS = 8
TOP_K = 2
EPS = 1e-6
N_DENSE = (DEPTH + 1) // 2
N_MOE = DEPTH // 2

Q_DIM = N_HEADS * HEAD_DIM
KV_DIM = N_KV_HEADS * HEAD_DIM
OFF_Q = 0
OFF_K = OFF_Q + Q_DIM
OFF_V = OFF_K + KV_DIM
OFF_CIN = OFF_V + KV_DIM
OFF_COUT = OFF_CIN + CONV_WIDTH
OFF_CX = OFF_COUT + CONV_WIDTH
OFF_GA = OFF_CX + CONV_WIDTH
OFF_GB = OFF_GA + D_MODEL
IN_DIM = OFF_GB + D_MODEL

kernel_name = "hybrid_dit_gqa_shortconv_moe"


def rmsnorm(x, g):
    x32 = x.astype(jnp.float32)
    y = x32 * lax.rsqrt(jnp.mean(x32 * x32, axis=-1, keepdims=True) + EPS)
    return (y * g.astype(jnp.float32)).astype(x.dtype)


def modulate(h, shift, scale):
    return h * (1 + scale) + shift


def axial_rope_tables(rows):
    row = jnp.repeat(jnp.arange(rows), GRID_W).astype(jnp.float32)
    col = jnp.tile(jnp.arange(GRID_W), rows).astype(jnp.float32)
    n_pairs_axis = HEAD_DIM // 4
    inv = ROPE_THETA ** (-jnp.arange(n_pairs_axis, dtype=jnp.float32) / n_pairs_axis)
    ang = jnp.concatenate([row[:, None] * inv[None, :], col[:, None] * inv[None, :]], axis=-1)
    return jnp.cos(ang), jnp.sin(ang)


def apply_rope(t, cos, sin):
    tr = t.reshape(*t.shape[:-1], HEAD_DIM // 2, 2)
    t0, t1 = tr[..., 0], tr[..., 1]
    c = cos[None, :, None, :].astype(t.dtype)
    s = sin[None, :, None, :].astype(t.dtype)
    return jnp.stack([t0 * c - t1 * s, t0 * s + t1 * c], axis=-1).reshape(t.shape)


def split_heads(t, n):
    return t.reshape(*t.shape[:-1], n, HEAD_DIM)


def gqa_block(qblk, k, v):
    s = jnp.einsum('bqkgd,bskd->bkgqs', qblk, k).astype(jnp.float32) * ATTN_SCALE
    p = jax.nn.softmax(s, axis=-1).astype(v.dtype)
    return jnp.einsum('bkgqs,bskd->bqkgd', p, v)


def latent_attention(q, k_all, v_all):
    b, s = q.shape[:2]
    nblk = s // Q_BLOCK
    qb = q.reshape(b, nblk, Q_BLOCK, N_KV_HEADS, GROUP, HEAD_DIM).transpose(1, 0, 2, 3, 4, 5)
    o = lax.map(lambda qblk: gqa_block(qblk, k_all, v_all), qb)
    return o.transpose(1, 0, 2, 3, 4, 5).reshape(b, s, Q_DIM)


def context_attention(q, k, v):
    b, s = q.shape[:2]
    qg = q.reshape(b, s, N_KV_HEADS, GROUP, HEAD_DIM)
    return gqa_block(qg, k, v).reshape(b, s, Q_DIM)


def project_qkv(p, g_q, g_k):
    q = rmsnorm(split_heads(p[..., OFF_Q:OFF_K], N_HEADS), g_q)
    k = rmsnorm(split_heads(p[..., OFF_K:OFF_V], N_KV_HEADS), g_k)
    v = split_heads(p[..., OFF_V:OFF_CIN], N_KV_HEADS)
    return q, k, v


def short_conv(u, w):
    return lax.conv_general_dilated(
        u, w[:, None, :].astype(u.dtype), window_strides=(1,),
        padding=((CONV_K // 2, CONV_K // 2),),
        dimension_numbers=('NWC', 'WIO', 'NWC'),
        feature_group_count=u.shape[-1])


def conv_branch(p, w_conv):
    gate_c = p[..., OFF_CIN:OFF_COUT]
    gate_b = p[..., OFF_COUT:OFF_CX]
    u = p[..., OFF_CX:OFF_GA]
    return gate_b * short_conv(gate_c * u, w_conv)


def merge_branches(p, attn, conv, w_pa, w_pb, w_o):
    ga = jax.nn.sigmoid(p[..., OFF_GA:OFF_GB])
    gb = jax.nn.sigmoid(p[..., OFF_GB:IN_DIM])
    return (ga * (attn @ w_pa) + gb * (conv @ w_pb)) @ w_o


def swiglu(h, wg, wu, wd):
    return (jax.nn.silu(h @ wg) * (h @ wu)) @ wd


def moe_swiglu(h, w_router, wg, wu, wd):
    logits = (h @ w_router).astype(jnp.float32)
    top_v, top_i = lax.top_k(logits, TOP_K)
    probs = jax.nn.softmax(top_v, axis=-1)
    gates = jnp.sum(jax.nn.one_hot(top_i, N_EXPERTS, dtype=jnp.float32) * probs[..., None], axis=-2)
    gates = gates.astype(h.dtype)
    out = jnp.zeros_like(h)
    for e in range(N_EXPERTS):
        out = out + gates[..., e:e + 1] * swiglu(h, wg[e], wu[e], wd[e])
    return out


def setup_inputs(seed: int = 0) -> dict:
    key = jax.random.key(seed)
    ks = jax.random.split(key, 24)
    f32 = jnp.float32
    D = D_MODEL
    nrm = lambda k, shape, s: jax.random.normal(k, shape, f32) * s
    gain = lambda k, shape: 1.0 + 0.05 * jax.random.normal(k, shape, f32)
    return {
        "x": nrm(ks[0], (BATCH, SEQ, D), 1.0),
        "c": nrm(ks[1], (BATCH, D), 1.0),
        "ctx": nrm(ks[2], (BATCH, CTX_LEN, D), 1.0),
        "c_ctx": nrm(ks[3], (D,), 1.0),
        "w_ada": nrm(ks[4], (DEPTH, D, 6 * D), 0.5 * D ** -0.5),
        "b_ada": nrm(ks[5], (DEPTH, 6 * D), 0.01),
        "g_pre_mix": gain(ks[6], (DEPTH, D)),
        "g_post_mix": gain(ks[7], (DEPTH, D)),
        "g_pre_ffn": gain(ks[8], (DEPTH, D)),
        "g_post_ffn": gain(ks[9], (DEPTH, D)),
        "w_in": nrm(ks[10], (DEPTH, D, IN_DIM), D ** -0.5),
        "g_q": gain(ks[11], (DEPTH, HEAD_DIM)),
        "g_k": gain(ks[12], (DEPTH, HEAD_DIM)),
        "w_conv": nrm(ks[13], (DEPTH, CONV_K, CONV_WIDTH), CONV_K ** -0.5),
        "w_pa": nrm(ks[14], (DEPTH, Q_DIM, D), Q_DIM ** -0.5),
        "w_pb": nrm(ks[15], (DEPTH, CONV_WIDTH, D), CONV_WIDTH ** -0.5),
        "w_o": nrm(ks[16], (DEPTH, D, D), D ** -0.5),
        "w_dense_gate": nrm(ks[17], (N_DENSE, D, D_FF), D ** -0.5),
        "w_dense_up": nrm(ks[18], (N_DENSE, D, D_FF), D ** -0.5),
        "w_dense_down": nrm(ks[19], (N_DENSE, D_FF, D), D_FF ** -0.5),
        "w_router": nrm(ks[20], (N_MOE, D, N_EXPERTS), D ** -0.5),
        "w_moe_gate": nrm(ks[21], (N_MOE, N_EXPERTS, D, D_FF), D ** -0.5),
        "w_moe_up": nrm(ks[22], (N_MOE, N_EXPERTS, D, D_FF), D ** -0.5),
        "w_moe_down": nrm(ks[23], (N_MOE, N_EXPERTS, D_FF, D), D_FF ** -0.5),
    }


def reference(x, c, ctx, c_ctx, w_ada, b_ada, g_pre_mix, g_post_mix, g_pre_ffn, g_post_ffn,
              w_in, g_q, g_k, w_conv, w_pa, w_pb, w_o,
              w_dense_gate, w_dense_up, w_dense_down,
              w_router, w_moe_gate, w_moe_up, w_moe_down):
    ROWS = x.shape[1] // GRID_W
    cos, sin = axial_rope_tables(ROWS)
    silu_c = jax.nn.silu(c)
    silu_cc = jax.nn.silu(c_ctx)

    for l in range(DEPTH):
        last = l == DEPTH - 1
        mod_x = (silu_c @ w_ada[l] + b_ada[l])[:, None, :]
        mod_c = (silu_cc @ w_ada[l] + b_ada[l])[None, None, :]
        sh1, sc1, gt1, sh2, sc2, gt2 = jnp.split(mod_x, 6, axis=-1)
        csh1, csc1, cgt1, csh2, csc2, cgt2 = jnp.split(mod_c, 6, axis=-1)

        hx = modulate(rmsnorm(x, g_pre_mix[l]), sh1, sc1)
        hc = modulate(rmsnorm(ctx, g_pre_mix[l]), csh1, csc1)
        px = hx @ w_in[l]
        qx, kx, vx = project_qkv(px, g_q[l], g_k[l])
        qx = apply_rope(qx, cos, sin)
        kx = apply_rope(kx, cos, sin)

        if last:
            pc_kv = hc @ w_in[l, :, OFF_K:OFF_CIN]
            kc = rmsnorm(split_heads(pc_kv[..., :KV_DIM], N_KV_HEADS), g_k[l])
            vc = split_heads(pc_kv[..., KV_DIM:], N_KV_HEADS)
        else:
            pc = hc @ w_in[l]
            qc, kc, vc = project_qkv(pc, g_q[l], g_k[l])

        k_all = jnp.concatenate([kc, kx], axis=1)
        v_all = jnp.concatenate([vc, vx], axis=1)
        attn_x = latent_attention(qx, k_all, v_all)
        conv_x = conv_branch(px, w_conv[l])
        mix_x = merge_branches(px, attn_x, conv_x, w_pa[l], w_pb[l], w_o[l])
        x = x + gt1 * rmsnorm(mix_x, g_post_mix[l])

        if not last:
            attn_c = context_attention(qc, kc, vc)
            conv_c = conv_branch(pc, w_conv[l])
            mix_c = merge_branches(pc, attn_c, conv_c, w_pa[l], w_pb[l], w_o[l])
            ctx = ctx + cgt1 * rmsnorm(mix_c, g_post_mix[l])

        fx = modulate(rmsnorm(x, g_pre_ffn[l]), sh2, sc2)
        if l % 2 == 0:
            i = l // 2
            yx = swiglu(fx, w_dense_gate[i], w_dense_up[i], w_dense_down[i])
        else:
            i = l // 2
            yx = moe_swiglu(fx, w_router[i], w_moe_gate[i], w_moe_up[i], w_moe_down[i])
        x = x + gt2 * rmsnorm(yx, g_post_ffn[l])

        if not last:
            fc = modulate(rmsnorm(ctx, g_pre_ffn[l]), csh2, csc2)
            if l % 2 == 0:
                yc = swiglu(fc, w_dense_gate[l // 2], w_dense_up[l // 2], w_dense_down[l // 2])
            else:
                yc = moe_swiglu(fc, w_router[l // 2], w_moe_gate[l // 2], w_moe_up[l // 2], w_moe_down[l // 2])
            ctx = ctx + cgt2 * rmsnorm(yc, g_post_ffn[l])

    return x
```

```python
import functools

import jax
import jax.numpy as jnp
from jax import lax
from jax.experimental import pallas as pl
from jax.experimental.pallas import tpu as pltpu

HEAD_DIM = 128
GROUP = 4
GRID_W = 64
N_EXPERTS = 8
EPS = 1e-6
ROPE_THETA = 10000.0

V7X_LANES = 128
V7X_VMEM_BYTES = 64 * 1024 * 1024
VMEM_RESERVE_BYTES = 8 * 1024 * 1024

F32 = jnp.float32
BF16 = jnp.bfloat16


def _params(semantics, est_bytes):
    limit = int(min(V7X_VMEM_BYTES - VMEM_RESERVE_BYTES, max(32 * 1024 * 1024, 2 * est_bytes)))
    return pltpu.CompilerParams(dimension_semantics=semantics, vmem_limit_bytes=limit)


def _tile(n, pref):
    t = min(n, pref)
    while n % t:
        t //= 2
    return t


def _rms(x32, g):
    ms = jnp.mean(x32 * x32, axis=-1, keepdims=True)
    return x32 * lax.rsqrt(ms + EPS) * g


def _sigmoid(x):
    return 1.0 / (1.0 + jnp.exp(-x))


NORM_CHUNK_ROWS = 256


def _prenorm_to(h_ref, x_ref, g_ref, sh_ref, sc_ref):
    tm = h_ref.shape[0]
    rows = min(tm, NORM_CHUNK_ROWS)

    def body(r, c):
        sl = pl.ds(pl.multiple_of(r * rows, rows), rows)
        y = _rms(x_ref[0, sl, :], g_ref[...])
        h_ref[sl, :] = (y * (1.0 + sc_ref[0]) + sh_ref[0]).astype(h_ref.dtype)
        return c

    lax.fori_loop(0, tm // rows, body, 0)


def _ada_kernel(c_ref, w_ref, b_ref, o_ref):
    c = c_ref[...]
    s = (c * _sigmoid(c)).astype(BF16)
    w = w_ref[0].astype(BF16)
    o_ref[0] = jnp.dot(s, w, preferred_element_type=F32) + b_ref[0]


def _ada(c_rows, w_ada, b_ada):
    depth, d, n = w_ada.shape
    r = c_rows.shape[0]
    tn = _tile(n, 1024)
    est = 2 * d * tn * 4 + 2 * r * tn * 4 + r * d * 4
    return pl.pallas_call(
        _ada_kernel,
        out_shape=jax.ShapeDtypeStruct((depth, r, n), F32),
        grid=(depth, n // tn),
        in_specs=[
            pl.BlockSpec((r, d), lambda l, j: (0, 0)),
            pl.BlockSpec((1, d, tn), lambda l, j: (l, 0, j)),
            pl.BlockSpec((1, 1, tn), lambda l, j: (l, 0, j)),
        ],
        out_specs=pl.BlockSpec((1, r, tn), lambda l, j: (l, 0, j)),
        compiler_params=_params(("parallel", "parallel"), est),
        name="ada_mod",
    )(c_rows, w_ada, b_ada.reshape(depth, 1, n))


def _inproj_kernel(x_ref, g_ref, sh_ref, sc_ref, w_ref, o_ref, h_ref):
    @pl.when(pl.program_id(2) == 0)
    def _():
        _prenorm_to(h_ref, x_ref, g_ref, sh_ref, sc_ref)

    o_ref[0] = jnp.dot(h_ref[...], w_ref[...], preferred_element_type=F32).astype(o_ref.dtype)


def _inproj(x, g, shift, scale, w):
    b, t, d = x.shape
    n = w.shape[1]
    tm = _tile(t, 1024)
    tn = _tile(n, 1024)
    per_b = shift.shape[0] > 1
    mod_map = (lambda bi, i, j: (bi, 0, 0)) if per_b else (lambda bi, i, j: (0, 0, 0))
    est = 2 * tm * d * 4 + tm * d * 2 + 2 * d * tn * 2 + 2 * tm * tn * 2 + 3 * tm * d * 4
    return pl.pallas_call(
        _inproj_kernel,
        out_shape=jax.ShapeDtypeStruct((b, t, n), BF16),
        grid=(b, t // tm, n // tn),
        in_specs=[
            pl.BlockSpec((1, tm, d), lambda bi, i, j: (bi, i, 0)),
            pl.BlockSpec((1, d), lambda bi, i, j: (0, 0)),
            pl.BlockSpec((1, 1, d), mod_map),
            pl.BlockSpec((1, 1, d), mod_map),
            pl.BlockSpec((d, tn), lambda bi, i, j: (0, j)),
        ],
        out_specs=pl.BlockSpec((1, tm, tn), lambda bi, i, j: (bi, i, j)),
        scratch_shapes=[pltpu.VMEM((tm, d), BF16)],
        compiler_params=_params(("parallel", "parallel", "arbitrary"), est),
        name="inproj",
    )(x, g.reshape(1, d), shift, scale, w)


def _qk_kernel(q_ref, k_ref, gq_ref, gk_ref, cc_ref, ss_ref, qo_ref, ko_ref, *, rope, q_scale):
    def head(t, g, scale):
        y = _rms(t.astype(F32), g)
        if rope:
            y = y * cc_ref[...] + pltpu.roll(y, HEAD_DIM // 2, 1) * ss_ref[...]
        if scale != 1.0:
            y = y * scale
        return y.astype(BF16)

    for hh in range(qo_ref.shape[1]):
        qo_ref[0, hh] = head(q_ref[0, :, hh * HEAD_DIM:(hh + 1) * HEAD_DIM], gq_ref[...], q_scale)
    for hh in range(ko_ref.shape[1]):
        ko_ref[0, hh] = head(k_ref[0, :, hh * HEAD_DIM:(hh + 1) * HEAD_DIM], gk_ref[...], 1.0)


def _qk_norm_rope(pq, q_blk, pk, k_blk, n_heads, n_kv, gq, gk, cc, ss, rope):
    b, t = pq.shape[:2]
    tm = _tile(t, 512)
    qw, kw = n_heads * HEAD_DIM, n_kv * HEAD_DIM
    est = 2 * tm * (qw + kw) * 2 * 2 + 6 * tm * HEAD_DIM * 4
    kern = functools.partial(_qk_kernel, rope=rope, q_scale=HEAD_DIM ** -0.5)
    return pl.pallas_call(
        kern,
        out_shape=(jax.ShapeDtypeStruct((b, n_heads, t, HEAD_DIM), BF16),
                   jax.ShapeDtypeStruct((b, n_kv, t, HEAD_DIM), BF16)),
        grid=(b, t // tm),
        in_specs=[
            pl.BlockSpec((1, tm, qw), lambda bi, i: (bi, i, q_blk)),
            pl.BlockSpec((1, tm, kw), lambda bi, i: (bi, i, k_blk)),
            pl.BlockSpec((1, HEAD_DIM), lambda bi, i: (0, 0)),
            pl.BlockSpec((1, HEAD_DIM), lambda bi, i: (0, 0)),
            pl.BlockSpec((tm, HEAD_DIM), lambda bi, i: (i, 0)),
            pl.BlockSpec((tm, HEAD_DIM), lambda bi, i: (i, 0)),
        ],
        out_specs=(pl.BlockSpec((1, n_heads, tm, HEAD_DIM), lambda bi, i: (bi, 0, i, 0)),
                   pl.BlockSpec((1, n_kv, tm, HEAD_DIM), lambda bi, i: (bi, 0, i, 0))),
        compiler_params=_params(("parallel", "parallel"), est),
        name="qk_norm_rope" if rope else "qk_norm",
    )(pq, pk, gq.reshape(1, HEAD_DIM), gk.reshape(1, HEAD_DIM), cc, ss)


def _attn_kernel(q_ref, kt_ref, v_ref, o_ref):
    g, tq = q_ref.shape[1], q_ref.shape[2]
    q = q_ref[0].reshape(g * tq, HEAD_DIM)
    s = jnp.dot(q, kt_ref[0, 0], preferred_element_type=F32)
    m = jnp.max(s, axis=-1, keepdims=True)
    p = jnp.exp(s - m)
    l = jnp.sum(p, axis=-1, keepdims=True)
    o = jnp.dot(p.astype(BF16), v_ref[0], preferred_element_type=F32) / l
    for gi in range(g):
        o_ref[0, :, gi * HEAD_DIM:(gi + 1) * HEAD_DIM] = o[gi * tq:(gi + 1) * tq].astype(o_ref.dtype)


def _attention(q, kt, v, v_blk0):
    b, h, t, _ = q.shape
    n_kv, l = kt.shape[1], kt.shape[3]
    tq = _tile(t, 256)
    gw = GROUP * HEAD_DIM
    est = 2 * (GROUP * tq * HEAD_DIM * 2 + 2 * HEAD_DIM * l * 2 + tq * gw * 2) + GROUP * tq * l * 10
    return pl.pallas_call(
        _attn_kernel,
        out_shape=jax.ShapeDtypeStruct((b, t, h * HEAD_DIM), BF16),
        grid=(b, n_kv, t // tq),
        in_specs=[
            pl.BlockSpec((1, GROUP, tq, HEAD_DIM), lambda bi, kv, i: (bi, kv, i, 0)),
            pl.BlockSpec((1, 1, HEAD_DIM, l), lambda bi, kv, i: (bi, kv, 0, 0)),
            pl.BlockSpec((1, l, HEAD_DIM), lambda bi, kv, i: (bi, 0, v_blk0 + kv)),
        ],
        out_specs=pl.BlockSpec((1, tq, gw), lambda bi, kv, i: (bi, i, kv)),
        compiler_params=_params(("parallel", "parallel", "parallel"), est),
        name="gqa_attention",
    )(q, kt, v)


def _aproj_kernel(a_ref, ga_ref, w_ref, o_ref):
    t = jnp.dot(a_ref[0], w_ref[...], preferred_element_type=F32)
    o_ref[0] = (_sigmoid(ga_ref[0].astype(F32)) * t).astype(o_ref.dtype)


def _aproj(attn, px, ga_blk, w_pa):
    b, t, d = attn.shape
    tm = _tile(t, 512)
    est = 2 * d * d * 2 + 2 * 3 * tm * d * 2 + 3 * tm * d * 4
    return pl.pallas_call(
        _aproj_kernel,
        out_shape=jax.ShapeDtypeStruct((b, t, d), BF16),
        grid=(b, t // tm),
        in_specs=[
            pl.BlockSpec((1, tm, d), lambda bi, i: (bi, i, 0)),
            pl.BlockSpec((1, tm, d), lambda bi, i: (bi, i, ga_blk)),
            pl.BlockSpec((d, d), lambda bi, i: (0, 0)),
        ],
        out_specs=pl.BlockSpec((1, tm, d), lambda bi, i: (bi, i, 0)),
        compiler_params=_params(("parallel", "parallel"), est),
        name="attn_proj",
    )(attn, px, w_pa)


HALO_ROWS = 16


def _bproj_kernel(cin_ref, cout_ref, cx_ref, cin_p_ref, cx_p_ref, cin_n_ref, cx_n_ref,
                  gb_ref, t1_ref, wc_ref, w_ref, o_ref):
    i, n = pl.program_id(1), pl.num_programs(1)
    tm = cin_ref.shape[1]
    z = cin_ref[0].astype(F32) * cx_ref[0].astype(F32)
    z_prev_row = cin_p_ref[0, HALO_ROWS - 1:HALO_ROWS, :].astype(F32) * cx_p_ref[0, HALO_ROWS - 1:HALO_ROWS, :].astype(F32)
    z_next_row = cin_n_ref[0, 0:1, :].astype(F32) * cx_n_ref[0, 0:1, :].astype(F32)
    z_prev_row = jnp.where(i > 0, z_prev_row, 0.0)
    z_next_row = jnp.where(i < n - 1, z_next_row, 0.0)
    row = lax.broadcasted_iota(jnp.int32, z.shape, 0)
    z_m1 = jnp.where(row == 0, z_prev_row, pltpu.roll(z, 1, 0))
    z_p1 = jnp.where(row == tm - 1, z_next_row, pltpu.roll(z, tm - 1, 0))
    conv = wc_ref[0:1, :] * z_m1 + wc_ref[1:2, :] * z + wc_ref[2:3, :] * z_p1
    cb = (cout_ref[0].astype(F32) * conv).astype(BF16)
    t = jnp.dot(cb, w_ref[...], preferred_element_type=F32)
    o_ref[0] = (t1_ref[0].astype(F32) + _sigmoid(gb_ref[0].astype(F32)) * t).astype(o_ref.dtype)


def _bproj(px, blks, t1, w_conv, w_pb):
    b, t, _ = px.shape
    d = t1.shape[2]
    tm = _tile(t, 512)
    hb = tm // HALO_ROWS
    n_h = t // HALO_ROWS
    cin_b, cout_b, cx_b, gb_b = blks

    def main(blk):
        return pl.BlockSpec((1, tm, d), lambda bi, i: (bi, i, blk))

    def prev(blk):
        return pl.BlockSpec((1, HALO_ROWS, d), lambda bi, i: (bi, jnp.maximum(i * hb - 1, 0), blk))

    def nxt(blk):
        return pl.BlockSpec((1, HALO_ROWS, d), lambda bi, i: (bi, jnp.minimum((i + 1) * hb, n_h - 1), blk))

    est = 2 * d * d * 2 + 2 * 6 * tm * d * 2 + 6 * tm * d * 4
    return pl.pallas_call(
        _bproj_kernel,
        out_shape=jax.ShapeDtypeStruct((b, t, d), BF16),
        grid=(b, t // tm),
        in_specs=[main(cin_b), main(cout_b), main(cx_b), prev(cin_b), prev(cx_b), nxt(cin_b), nxt(cx_b),
                  main(gb_b),
                  pl.BlockSpec((1, tm, d), lambda bi, i: (bi, i, 0)),
                  pl.BlockSpec((3, d), lambda bi, i: (0, 0)),
                  pl.BlockSpec((d, d), lambda bi, i: (0, 0))],
        out_specs=pl.BlockSpec((1, tm, d), lambda bi, i: (bi, i, 0)),
        compiler_params=_params(("parallel", "parallel"), est),
        name="conv_proj",
    )(px, px, px, px, px, px, px, px, t1, w_conv, w_pb)


def _oproj_kernel(m_ref, w_ref, x_ref, gt_ref, g_ref, o_ref):
    mix = jnp.dot(m_ref[0], w_ref[...], preferred_element_type=F32)
    o_ref[0] = x_ref[0] + gt_ref[0] * _rms(mix, g_ref[...])


def _oproj(m, w_o, x, gate, g_post):
    b, t, d = x.shape
    tm = _tile(t, 512)
    per_b = gate.shape[0] > 1
    gmap = (lambda bi, i: (bi, 0, 0)) if per_b else (lambda bi, i: (0, 0, 0))
    est = 2 * d * d * 2 + 2 * tm * d * (2 + 4 + 4) + 3 * tm * d * 4
    return pl.pallas_call(
        _oproj_kernel,
        out_shape=jax.ShapeDtypeStruct((b, t, d), F32),
        grid=(b, t // tm),
        in_specs=[
            pl.BlockSpec((1, tm, d), lambda bi, i: (bi, i, 0)),
            pl.BlockSpec((d, d), lambda bi, i: (0, 0)),
            pl.BlockSpec((1, tm, d), lambda bi, i: (bi, i, 0)),
            pl.BlockSpec((1, 1, d), gmap),
            pl.BlockSpec((1, d), lambda bi, i: (0, 0)),
        ],
        out_specs=pl.BlockSpec((1, tm, d), lambda bi, i: (bi, i, 0)),
        compiler_params=_params(("parallel", "parallel"), est),
        name="out_proj",
    )(m, w_o, x, gate, g_post.reshape(1, d))


def _swiglu_step(h_ref, wg_ref, wu_ref, wd_ref, acc_ref, first):
    h = h_ref[...]
    g = jnp.dot(h, wg_ref[...], preferred_element_type=F32)
    u = jnp.dot(h, wu_ref[...], preferred_element_type=F32)
    a = (g * _sigmoid(g) * u).astype(BF16)
    part = jnp.dot(a, wd_ref[...], preferred_element_type=F32)

    @pl.when(first)
    def _():
        acc_ref[...] = part

    @pl.when(jnp.logical_not(first))
    def _():
        acc_ref[...] += part


def _ffn_kernel(x_ref, g1_ref, sh_ref, sc_ref, gt_ref, g2_ref, wg_ref, wu_ref, wd_ref, o_ref, h_ref, acc_ref):
    j, nj = pl.program_id(2), pl.num_programs(2)

    @pl.when(j == 0)
    def _():
        _prenorm_to(h_ref, x_ref, g1_ref, sh_ref, sc_ref)

    _swiglu_step(h_ref, wg_ref, wu_ref, wd_ref, acc_ref, j == 0)

    @pl.when(j == nj - 1)
    def _():
        o_ref[0] = x_ref[0] + gt_ref[0] * _rms(acc_ref[...], g2_ref[...])


def _ffn_dense(x, g_pre, shift, scale, gate, g_post, wg, wu, wd):
    b, t, d = x.shape
    f = wg.shape[1]
    tm = _tile(t, 512)
    tf = _tile(f, 512)
    per_b = gate.shape[0] > 1
    mmap = (lambda bi, i, j: (bi, 0, 0)) if per_b else (lambda bi, i, j: (0, 0, 0))
    vec = pl.BlockSpec((1, d), lambda bi, i, j: (0, 0))
    mod = pl.BlockSpec((1, 1, d), mmap)
    est = 4 * tm * d * 4 + tm * d * 2 + tm * d * 4 + 2 * 3 * d * tf * 2 + 4 * tm * tf * 4 + 2 * tm * d * 4
    return pl.pallas_call(
        _ffn_kernel,
        out_shape=jax.ShapeDtypeStruct((b, t, d), F32),
        grid=(b, t // tm, f // tf),
        in_specs=[
            pl.BlockSpec((1, tm, d), lambda bi, i, j: (bi, i, 0)),
            vec, mod, mod, mod, vec,
            pl.BlockSpec((d, tf), lambda bi, i, j: (0, j)),
            pl.BlockSpec((d, tf), lambda bi, i, j: (0, j)),
            pl.BlockSpec((tf, d), lambda bi, i, j: (j, 0)),
        ],
        out_specs=pl.BlockSpec((1, tm, d), lambda bi, i, j: (bi, i, 0)),
        scratch_shapes=[pltpu.VMEM((tm, d), BF16), pltpu.VMEM((tm, d), F32)],
        compiler_params=_params(("parallel", "parallel", "arbitrary"), est),
        name="ffn_dense",
    )(x, g_pre.reshape(1, d), shift, scale, gate, g_post.reshape(1, d), wg, wu, wd)


def _router_kernel(x_ref, g_ref, sh_ref, sc_ref, wr_ref, h_ref, w_ref, i_ref):
    y = _rms(x_ref[0], g_ref[...])
    h = y * (1.0 + sc_ref[0]) + sh_ref[0]
    h_ref[...] = h
    logits = jnp.dot(h, wr_ref[...], preferred_element_type=F32, precision=lax.Precision.HIGHEST)
    lane = lax.broadcasted_iota(jnp.int32, logits.shape, 1)
    lane_f = lane.astype(F32)
    neg = jnp.float32(-jnp.inf)
    big = jnp.float32(V7X_LANES)
    lg = jnp.where(lane < N_EXPERTS, logits, neg)
    v1 = jnp.max(lg, axis=-1, keepdims=True)
    i1 = jnp.min(jnp.where(lg == v1, lane_f, big), axis=-1, keepdims=True)
    lg2 = jnp.where(lane_f == i1, neg, lg)
    v2 = jnp.max(lg2, axis=-1, keepdims=True)
    i2 = jnp.min(jnp.where(lg2 == v2, lane_f, big), axis=-1, keepdims=True)
    e = jnp.exp(v2 - v1)
    den = 1.0 + e
    w_ref[...] = jnp.where(lane == 0, 1.0 / den, jnp.where(lane == 1, e / den, 0.0))
    i_ref[...] = jnp.where(lane == 0, i1, jnp.where(lane == 1, i2, 0.0)).astype(jnp.int32)


def _router(x, g_pre, shift, scale, w_router_pad):
    b, t, d = x.shape
    tm = _tile(t, 512)
    nt = t // tm
    est = 2 * tm * d * 4 * 2 + 2 * d * V7X_LANES * 4 + 4 * tm * d * 4
    vec = pl.BlockSpec((1, d), lambda bi, i: (0, 0))
    mod = pl.BlockSpec((1, 1, d), lambda bi, i: (bi, 0, 0))
    row = lambda w: pl.BlockSpec((tm, w), lambda bi, i: (bi * nt + i, 0))
    return pl.pallas_call(
        _router_kernel,
        out_shape=(jax.ShapeDtypeStruct((b * t, d), F32),
                   jax.ShapeDtypeStruct((b * t, V7X_LANES), F32),
                   jax.ShapeDtypeStruct((b * t, V7X_LANES), jnp.int32)),
        grid=(b, nt),
        in_specs=[pl.BlockSpec((1, tm, d), lambda bi, i: (bi, i, 0)), vec, mod, mod,
                  pl.BlockSpec((d, V7X_LANES), lambda bi, i: (0, 0))],
        out_specs=(row(d), row(V7X_LANES), row(V7X_LANES)),
        compiler_params=_params(("parallel", "parallel"), est),
        name="moe_router",
    )(x, g_pre.reshape(1, d), shift, scale, w_router_pad)


def _row_copy(src, src_row, dst, dst_row, sem):
    return pltpu.make_async_copy(src.at[pl.ds(src_row, 1)], dst.at[pl.ds(dst_row, 1)], sem)


def _dispatch_kernel(p1_ref, p2_ref, h_ref, xs_in_ref, xs_ref, sem):
    del xs_in_ref
    tm = h_ref.shape[0]
    base = pl.program_id(0) * tm

    def start(r, c):
        _row_copy(h_ref, r, xs_ref, p1_ref[base + r], sem).start()
        _row_copy(h_ref, r, xs_ref, p2_ref[base + r], sem).start()
        return c

    def wait(r, c):
        _row_copy(h_ref, 0, xs_ref, 0, sem).wait()
        _row_copy(h_ref, 0, xs_ref, 0, sem).wait()
        return c

    lax.fori_loop(0, tm, start, 0)
    lax.fori_loop(0, tm, wait, 0)


def _dispatch(h, p1, p2, n_rows):
    n, d = h.shape
    tm = _tile(n, 512)
    xs0 = jnp.zeros((n_rows, d), F32)
    return pl.pallas_call(
        _dispatch_kernel,
        out_shape=jax.ShapeDtypeStruct((n_rows, d), F32),
        grid_spec=pltpu.PrefetchScalarGridSpec(
            num_scalar_prefetch=2,
            grid=(n // tm,),
            in_specs=[pl.BlockSpec((tm, d), lambda i, p1, p2: (i, 0)),
                      pl.BlockSpec(memory_space=pl.ANY)],
            out_specs=pl.BlockSpec(memory_space=pl.ANY),
            scratch_shapes=[pltpu.SemaphoreType.DMA],
        ),
        input_output_aliases={3: 0},
        compiler_params=_params(("arbitrary",), 2 * tm * d * 4),
        name="moe_dispatch",
    )(p1, p2, h, xs0)


def _moe_kernel(te_ref, tv_ref, xs_ref, wg_ref, wu_ref, wd_ref, ys_ref, h_ref, acc_ref):
    i, j, nj = pl.program_id(0), pl.program_id(1), pl.num_programs(1)
    valid = tv_ref[i] > 0

    @pl.when(jnp.logical_and(valid, j == 0))
    def _():
        h_ref[...] = xs_ref[...].astype(BF16)

    @pl.when(valid)
    def _():
        _swiglu_step(h_ref, wg_ref.at[0], wu_ref.at[0], wd_ref.at[0], acc_ref, j == 0)

    @pl.when(j == nj - 1)
    def _():
        @pl.when(valid)
        def _():
            ys_ref[...] = acc_ref[...]

        @pl.when(jnp.logical_not(valid))
        def _():
            ys_ref[...] = jnp.zeros_like(ys_ref)


def _moe_experts(xs, tile_e, tile_v, wg, wu, wd, tm):
    r, d = xs.shape
    f = wg.shape[2]
    tf = _tile(f, 512)
    nf = f // tf

    def fblk(i, j, te, tv):
        return jnp.where(tv[i] > 0, j, nf - 1)

    est = 2 * tm * d * 4 + tm * d * 2 + tm * d * 4 + 2 * 3 * d * tf * 2 + 4 * tm * tf * 4 + 2 * tm * d * 4
    return pl.pallas_call(
        _moe_kernel,
        out_shape=jax.ShapeDtypeStruct((r, d), F32),
        grid_spec=pltpu.PrefetchScalarGridSpec(
            num_scalar_prefetch=2,
            grid=(r // tm, nf),
            in_specs=[pl.BlockSpec((tm, d), lambda i, j, te, tv: (i, 0)),
                      pl.BlockSpec((1, d, tf), lambda i, j, te, tv: (te[i], 0, fblk(i, j, te, tv))),
                      pl.BlockSpec((1, d, tf), lambda i, j, te, tv: (te[i], 0, fblk(i, j, te, tv))),
                      pl.BlockSpec((1, tf, d), lambda i, j, te, tv: (te[i], fblk(i, j, te, tv), 0))],
            out_specs=pl.BlockSpec((tm, d), lambda i, j, te, tv: (i, 0)),
            scratch_shapes=[pltpu.VMEM((tm, d), BF16), pltpu.VMEM((tm, d), F32)],
        ),
        compiler_params=_params(("arbitrary", "arbitrary"), est),
        name="moe_experts",
    )(tile_e, tile_v, xs, wg, wu, wd)


def _combine_kernel(p1_ref, p2_ref, ys_ref, x_ref, w_ref, gt_ref, g_ref, o_ref, ybuf, sem):
    i, n = pl.program_id(0), pl.num_programs(0)
    tm = x_ref.shape[0]

    def issue(step, slot):
        base = step * tm

        def body(r, c):
            _row_copy(ys_ref, p1_ref[base + r], ybuf.at[slot, 0], r, sem.at[slot]).start()
            _row_copy(ys_ref, p2_ref[base + r], ybuf.at[slot, 1], r, sem.at[slot]).start()
            return c

        lax.fori_loop(0, tm, body, 0)

    @pl.when(i == 0)
    def _():
        issue(0, 0)

    @pl.when(i + 1 < n)
    def _():
        issue(i + 1, (i + 1) % 2)

    slot = i % 2

    def wait(r, c):
        _row_copy(ys_ref, 0, ybuf.at[slot, 0], 0, sem.at[slot]).wait()
        _row_copy(ys_ref, 0, ybuf.at[slot, 1], 0, sem.at[slot]).wait()
        return c

    lax.fori_loop(0, tm, wait, 0)
    w = w_ref[...]
    y = w[:, 0:1] * ybuf[slot, 0] + w[:, 1:2] * ybuf[slot, 1]
    o_ref[...] = x_ref[...] + gt_ref[0] * _rms(y, g_ref[...])


def _combine(ys, p1, p2, x2d, topw, gate, g_post, tokens_per_batch):
    n, d = x2d.shape
    tm = _tile(tokens_per_batch, 512)
    nt = tokens_per_batch // tm
    est = 4 * tm * d * 4 + 4 * tm * d * 4 + 3 * tm * d * 4
    return pl.pallas_call(
        _combine_kernel,
        out_shape=jax.ShapeDtypeStruct((n, d), F32),
        grid_spec=pltpu.PrefetchScalarGridSpec(
            num_scalar_prefetch=2,
            grid=(n // tm,),
            in_specs=[pl.BlockSpec(memory_space=pl.ANY),
                      pl.BlockSpec((tm, d), lambda i, p1, p2: (i, 0)),
                      pl.BlockSpec((tm, V7X_LANES), lambda i, p1, p2: (i, 0)),
                      pl.BlockSpec((1, 1, d), lambda i, p1, p2: (i // nt, 0, 0)),
                      pl.BlockSpec((1, d), lambda i, p1, p2: (0, 0))],
            out_specs=pl.BlockSpec((tm, d), lambda i, p1, p2: (i, 0)),
            scratch_shapes=[pltpu.VMEM((2, 2, tm, d), F32), pltpu.SemaphoreType.DMA((2,))],
        ),
        compiler_params=_params(("arbitrary",), est),
        name="moe_combine",
    )(p1, p2, ys, x2d, topw, gate, g_post.reshape(1, d))


def _route_plan(e1, e2, tm, n_tiles):
    ar = jnp.arange(N_EXPERTS, dtype=jnp.int32)
    cnt = (e1[:, None] == ar).astype(jnp.int32) + (e2[:, None] == ar).astype(jnp.int32)
    csum = jnp.cumsum(cnt, axis=0)
    padded = ((csum[-1] + tm - 1) // tm) * tm
    ends = jnp.cumsum(padded)
    pos = (ends - padded)[None, :] + csum - 1
    p1 = jnp.take_along_axis(pos, e1[:, None], axis=1)[:, 0]
    p2 = jnp.take_along_axis(pos, e2[:, None], axis=1)[:, 0]
    starts = jnp.arange(n_tiles, dtype=jnp.int32) * tm
    tile_e = jnp.sum((starts[:, None] >= ends[None, :]).astype(jnp.int32), axis=1)
    tile_v = (starts < ends[-1]).astype(jnp.int32)
    last_e = jnp.max(jnp.where(tile_v > 0, tile_e, 0))
    tile_e = jnp.where(tile_v > 0, tile_e, last_e)
    return p1.astype(jnp.int32), p2.astype(jnp.int32), tile_e.astype(jnp.int32), tile_v


def _ffn_moe(x, g_pre, shift, scale, gate, g_post, w_router, wg, wu, wd):
    b, t, d = x.shape
    n = b * t
    wr = jnp.zeros((d, V7X_LANES), F32).at[:, :N_EXPERTS].set(w_router)
    h, topw, topi = _router(x, g_pre, shift, scale, wr)
    tm = _tile(n, 512)
    n_tiles = (2 * n) // tm + N_EXPERTS
    p1, p2, tile_e, tile_v = _route_plan(topi[:, 0], topi[:, 1], tm, n_tiles)
    xs = _dispatch(h, p1, p2, n_tiles * tm)
    ys = _moe_experts(xs, tile_e, tile_v, wg, wu, wd, tm)
    out = _combine(ys, p1, p2, x.reshape(n, d), topw, gate, g_post, t)
    return out.reshape(b, t, d)


def _in_perm(d):
    kv = d // GROUP
    head = jnp.concatenate([jnp.arange(0, HEAD_DIM, 2), jnp.arange(1, HEAD_DIM, 2)])
    q = (jnp.arange(d // HEAD_DIM)[:, None] * HEAD_DIM + head[None, :]).reshape(-1)
    k = d + (jnp.arange(kv // HEAD_DIM)[:, None] * HEAD_DIM + head[None, :]).reshape(-1)
    v = d + kv + jnp.arange(kv)
    rest = d + 2 * kv + jnp.arange(5 * d)
    return jnp.concatenate([q, rest, k, v]), head


def _rope_tables(t):
    rows = t // GRID_W
    row = jnp.repeat(jnp.arange(rows), GRID_W).astype(F32)
    col = jnp.tile(jnp.arange(GRID_W), rows).astype(F32)
    n_pairs_axis = HEAD_DIM // 4
    inv = ROPE_THETA ** (-jnp.arange(n_pairs_axis, dtype=F32) / n_pairs_axis)
    ang = jnp.concatenate([row[:, None] * inv[None, :], col[:, None] * inv[None, :]], axis=-1)
    cos, sin = jnp.cos(ang), jnp.sin(ang)
    return jnp.concatenate([cos, cos], axis=-1), jnp.concatenate([-sin, sin], axis=-1)


def _mixer(tok, mod, l, last_unused, w_in_l, gq, gk, w_conv_l, w_pa_l, w_pb_l, w_o_l, g_pre, g_post,
           cc, ss, rope, extra_k=None, extra_v=None):
    del l, last_unused
    b, t, d = tok.shape
    n_heads, n_kv = d // HEAD_DIM, d // HEAD_DIM // GROUP
    kv = n_kv * HEAD_DIM
    sh1, sc1, gt1 = mod
    px = _inproj(tok, g_pre, sh1, sc1, w_in_l)
    k_blk, v_blk = 6 * d // kv, (6 * d + kv) // HEAD_DIM
    q_r, k_r = _qk_norm_rope(px, 0, px, k_blk, n_heads, n_kv, gq, gk, cc, ss, rope)
    if extra_k is not None:
        k_all = jnp.concatenate([extra_k, k_r], axis=2)
        v_all = jnp.concatenate([extra_v, px[:, :, 6 * d + kv:]], axis=1)
        attn = _attention(q_r, jnp.swapaxes(k_all, 2, 3), v_all, 0)
    else:
        attn = _attention(q_r, jnp.swapaxes(k_r, 2, 3), px, v_blk)
    t1 = _aproj(attn, px, 4, w_pa_l)
    m = _bproj(px, (1, 2, 3, 5), t1, w_conv_l, w_pb_l)
    out = _oproj(m, w_o_l, tok, gt1, g_post)
    return out, k_r, px


def kernel(x, c, ctx, c_ctx, w_ada, b_ada, g_pre_mix, g_post_mix, g_pre_ffn, g_post_ffn, w_in, g_q, g_k,
           w_conv, w_pa, w_pb, w_o, w_dense_gate, w_dense_up, w_dense_down, w_router, w_moe_gate, w_moe_up,
           w_moe_down):
    b, t, d = x.shape
    c_len = ctx.shape[1]
    depth = w_in.shape[0]
    n_kv = d // HEAD_DIM // GROUP
    kv = n_kv * HEAD_DIM

    perm, head_perm = _in_perm(d)
    w_in_p = jnp.take(w_in, perm, axis=2).astype(BF16)
    gq_p, gk_p = g_q[:, head_perm], g_k[:, head_perm]
    cc, ss = _rope_tables(t)
    cc_c, ss_c = cc[:c_len], ss[:c_len]

    n_rows = -(-(b + 1) // 16) * 16
    c_rows = jnp.zeros((n_rows, d), F32).at[:b].set(c).at[b].set(c_ctx)
    mods = _ada(c_rows, w_ada, b_ada)

    for l in range(depth):
        last = l == depth - 1
        mx = [mods[l, :b, i * d:(i + 1) * d].reshape(b, 1, d) for i in range(6)]
        mc = [mods[l, b:b + 1, i * d:(i + 1) * d].reshape(1, 1, d) for i in range(6)]
        wpa, wpb, wo = w_pa[l].astype(BF16), w_pb[l].astype(BF16), w_o[l].astype(BF16)

        if last:
            pc_kv = _inproj(ctx, g_pre_mix[l], mc[0], mc[1], w_in_p[l][:, 6 * d:])
            _, kc = _qk_norm_rope(pc_kv, 0, pc_kv, 0, n_kv, n_kv, gk_p[l], gk_p[l], cc_c, ss_c, False)
            vc = pc_kv[:, :, kv:]
            ctx_new = None
        else:
            ctx_new, kc, pc = _mixer(ctx, (mc[0], mc[1], mc[2]), l, None, w_in_p[l], gq_p[l], gk_p[l], w_conv[l],
                                     wpa, wpb, wo, g_pre_mix[l], g_post_mix[l], cc_c, ss_c, False)
            vc = pc[:, :, 6 * d + kv:]
        x, _, _ = _mixer(x, (mx[0], mx[1], mx[2]), l, None, w_in_p[l], gq_p[l], gk_p[l], w_conv[l],
                         wpa, wpb, wo, g_pre_mix[l], g_post_mix[l], cc, ss, True, extra_k=kc, extra_v=vc)
        if not last:
            ctx = ctx_new

        i = l // 2
        if l % 2 == 0:
            wg, wu, wd = (w_dense_gate[i].astype(BF16), w_dense_up[i].astype(BF16), w_dense_down[i].astype(BF16))
            x = _ffn_dense(x, g_pre_ffn[l], mx[3], mx[4], mx[5], g_post_ffn[l], wg, wu, wd)
            if not last:
                ctx = _ffn_dense(ctx, g_pre_ffn[l], mc[3], mc[4], mc[5], g_post_ffn[l], wg, wu, wd)
        else:
            wg, wu, wd = (w_moe_gate[i].astype(BF16), w_moe_up[i].astype(BF16), w_moe_down[i].astype(BF16))
            x = _ffn_moe(x, g_pre_ffn[l], mx[3], mx[4], mx[5], g_post_ffn[l], w_router[i], wg, wu, wd)
            if not last:
                ctx = _ffn_moe(ctx, g_pre_ffn[l], jnp.broadcast_to(mc[3], (b, 1, d)), jnp.broadcast_to(mc[4], (b, 1, d)),
                               jnp.broadcast_to(mc[5], (b, 1, d)), g_post_ffn[l], w_router[i], wg, wu, wd)
    return x
```

```python
import functools

import jax
import jax.numpy as jnp
from jax import lax
from jax.experimental import pallas as pl
from jax.experimental.pallas import tpu as pltpu

HEAD_DIM = 128
GROUP = 4
GRID_W = 64
N_EXPERTS = 8
EPS = 1e-6
ROPE_THETA = 10000.0
LOG2_E = 1.4426950408889634

V7X_LANES = 128
V7X_VMEM_BYTES = 64 * 1024 * 1024
VMEM_RESERVE_BYTES = 8 * 1024 * 1024

F32 = jnp.float32
BF16 = jnp.bfloat16


def _params(semantics, est_bytes):
    limit = int(min(V7X_VMEM_BYTES - VMEM_RESERVE_BYTES, max(32 * 1024 * 1024, 2 * est_bytes)))
    return pltpu.CompilerParams(dimension_semantics=semantics, vmem_limit_bytes=limit)


def _tile(n, pref):
    t = min(n, pref)
    while n % t:
        t //= 2
    return t


def _rms(x32, g):
    ms = jnp.mean(x32 * x32, axis=-1, keepdims=True)
    return x32 * lax.rsqrt(ms + EPS) * g


def _sigmoid(x):
    return 1.0 / (1.0 + jnp.exp(-x))


NORM_CHUNK_ROWS = 256


def _prenorm_to(h_ref, x_ref, g_ref, sh_ref, sc_ref):
    tm = h_ref.shape[0]
    rows = min(tm, NORM_CHUNK_ROWS)

    def body(r, c):
        sl = pl.ds(pl.multiple_of(r * rows, rows), rows)
        y = _rms(x_ref[0, sl, :], g_ref[...])
        h_ref[sl, :] = (y * (1.0 + sc_ref[0]) + sh_ref[0]).astype(h_ref.dtype)
        return c

    lax.fori_loop(0, tm // rows, body, 0)


def _ada_kernel(c_ref, w_ref, b_ref, o_ref):
    c = c_ref[...]
    s = (c * _sigmoid(c)).astype(BF16)
    w = w_ref[0].astype(BF16)
    o_ref[0] = jnp.dot(s, w, preferred_element_type=F32) + b_ref[0]


def _ada(c_rows, w_ada, b_ada):
    depth, d, n = w_ada.shape
    r = c_rows.shape[0]
    tn = _tile(n, 1024)
    est = 2 * d * tn * 4 + 2 * r * tn * 4 + r * d * 4
    return pl.pallas_call(
        _ada_kernel,
        out_shape=jax.ShapeDtypeStruct((depth, r, n), F32),
        grid=(depth, n // tn),
        in_specs=[
            pl.BlockSpec((r, d), lambda l, j: (0, 0)),
            pl.BlockSpec((1, d, tn), lambda l, j: (l, 0, j)),
            pl.BlockSpec((1, 1, tn), lambda l, j: (l, 0, j)),
        ],
        out_specs=pl.BlockSpec((1, r, tn), lambda l, j: (l, 0, j)),
        compiler_params=_params(("parallel", "parallel"), est),
        name="ada_mod",
    )(c_rows, w_ada, b_ada.reshape(depth, 1, n))


def _inproj_kernel(x_ref, g_ref, sh_ref, sc_ref, w_ref, o_ref, h_ref):
    @pl.when(pl.program_id(2) == 0)
    def _():
        _prenorm_to(h_ref, x_ref, g_ref, sh_ref, sc_ref)

    o_ref[0] = jnp.dot(h_ref[...], w_ref[...], preferred_element_type=F32).astype(o_ref.dtype)


def _inproj(x, g, shift, scale, w):
    b, t, d = x.shape
    n = w.shape[1]
    tm = _tile(t, 1024)
    tn = _tile(n, 1024)
    per_b = shift.shape[0] > 1
    mod_map = (lambda bi, i, j: (bi, 0, 0)) if per_b else (lambda bi, i, j: (0, 0, 0))
    est = 2 * tm * d * 4 + tm * d * 2 + 2 * d * tn * 2 + 2 * tm * tn * 2 + 3 * tm * d * 4
    return pl.pallas_call(
        _inproj_kernel,
        out_shape=jax.ShapeDtypeStruct((b, t, n), BF16),
        grid=(b, t // tm, n // tn),
        in_specs=[
            pl.BlockSpec((1, tm, d), lambda bi, i, j: (bi, i, 0)),
            pl.BlockSpec((1, d), lambda bi, i, j: (0, 0)),
            pl.BlockSpec((1, 1, d), mod_map),
            pl.BlockSpec((1, 1, d), mod_map),
            pl.BlockSpec((d, tn), lambda bi, i, j: (0, j)),
        ],
        out_specs=pl.BlockSpec((1, tm, tn), lambda bi, i, j: (bi, i, j)),
        scratch_shapes=[pltpu.VMEM((tm, d), BF16)],
        compiler_params=_params(("parallel", "parallel", "arbitrary"), est),
        name="inproj",
    )(x, g.reshape(1, d), shift, scale, w)


def _qk_kernel(q_ref, k_ref, gq_ref, gk_ref, cc_ref, ss_ref, qo_ref, ko_ref, *, rope, q_scale):
    def head(t, g, scale):
        y = _rms(t.astype(F32), g)
        if rope:
            y = y * cc_ref[...] + pltpu.roll(y, HEAD_DIM // 2, 1) * ss_ref[...]
        if scale != 1.0:
            y = y * scale
        return y.astype(BF16)

    for hh in range(qo_ref.shape[1]):
        qo_ref[0, hh] = head(q_ref[0, :, hh * HEAD_DIM:(hh + 1) * HEAD_DIM], gq_ref[...], q_scale)
    for hh in range(ko_ref.shape[1]):
        ko_ref[0, hh] = head(k_ref[0, :, hh * HEAD_DIM:(hh + 1) * HEAD_DIM], gk_ref[...], 1.0)


def _qk_norm_rope(pq, q_blk, pk, k_blk, n_heads, n_kv, gq, gk, cc, ss, rope):
    b, t = pq.shape[:2]
    tm = _tile(t, 512)
    qw, kw = n_heads * HEAD_DIM, n_kv * HEAD_DIM
    est = 2 * tm * (qw + kw) * 2 * 2 + 6 * tm * HEAD_DIM * 4
    kern = functools.partial(_qk_kernel, rope=rope, q_scale=LOG2_E * HEAD_DIM ** -0.5)
    return pl.pallas_call(
        kern,
        out_shape=(jax.ShapeDtypeStruct((b, n_heads, t, HEAD_DIM), BF16),
                   jax.ShapeDtypeStruct((b, n_kv, t, HEAD_DIM), BF16)),
        grid=(b, t // tm),
        in_specs=[
            pl.BlockSpec((1, tm, qw), lambda bi, i: (bi, i, q_blk)),
            pl.BlockSpec((1, tm, kw), lambda bi, i: (bi, i, k_blk)),
            pl.BlockSpec((1, HEAD_DIM), lambda bi, i: (0, 0)),
            pl.BlockSpec((1, HEAD_DIM), lambda bi, i: (0, 0)),
            pl.BlockSpec((tm, HEAD_DIM), lambda bi, i: (i, 0)),
            pl.BlockSpec((tm, HEAD_DIM), lambda bi, i: (i, 0)),
        ],
        out_specs=(pl.BlockSpec((1, n_heads, tm, HEAD_DIM), lambda bi, i: (bi, 0, i, 0)),
                   pl.BlockSpec((1, n_kv, tm, HEAD_DIM), lambda bi, i: (bi, 0, i, 0))),
        compiler_params=_params(("parallel", "parallel"), est),
        name="qk_norm_rope" if rope else "qk_norm",
    )(pq, pk, gq.reshape(1, HEAD_DIM), gk.reshape(1, HEAD_DIM), cc, ss)


def _attn_kernel(q_ref, kt_ref, v_ref, o_ref):
    kt, v = kt_ref[0, 0], v_ref[0, 0]
    for gi in range(q_ref.shape[1]):
        s = jnp.dot(q_ref[0, gi], kt, preferred_element_type=F32)
        m = jnp.max(s, axis=-1, keepdims=True)
        p = jnp.exp2(s - m).astype(BF16)
        o = jnp.dot(p, v, preferred_element_type=F32)
        o_ref[0, :, gi * HEAD_DIM:(gi + 1) * HEAD_DIM] = (
            o[:, :HEAD_DIM] / o[:, HEAD_DIM:HEAD_DIM + 1]).astype(o_ref.dtype)


def _attention(q, kt, v_ext):
    b, h, t, _ = q.shape
    n_kv, l = kt.shape[1], kt.shape[3]
    tq = _tile(t, 512)
    gw = GROUP * HEAD_DIM
    est = 2 * (GROUP * tq * HEAD_DIM * 2 + 3 * HEAD_DIM * l * 2 + tq * gw * 2) + GROUP * tq * l * 8
    return pl.pallas_call(
        _attn_kernel,
        out_shape=jax.ShapeDtypeStruct((b, t, h * HEAD_DIM), BF16),
        grid=(b, n_kv, t // tq),
        in_specs=[
            pl.BlockSpec((1, GROUP, tq, HEAD_DIM), lambda bi, kv, i: (bi, kv, i, 0)),
            pl.BlockSpec((1, 1, HEAD_DIM, l), lambda bi, kv, i: (bi, kv, 0, 0)),
            pl.BlockSpec((1, 1, l, 2 * HEAD_DIM), lambda bi, kv, i: (bi, kv, 0, 0)),
        ],
        out_specs=pl.BlockSpec((1, tq, gw), lambda bi, kv, i: (bi, i, kv)),
        compiler_params=_params(("parallel", "parallel", "parallel"), est),
        name="gqa_attention",
    )(q, kt, v_ext)


def _v_ext(v, n_kv):
    b, l, _ = v.shape
    vh = jnp.swapaxes(v.reshape(b, l, n_kv, HEAD_DIM), 1, 2)
    return jnp.concatenate([vh, jnp.ones_like(vh)], axis=-1)


def _aproj_kernel(a_ref, ga_ref, w_ref, o_ref):
    t = jnp.dot(a_ref[0], w_ref[...], preferred_element_type=F32)
    o_ref[0] = (_sigmoid(ga_ref[0].astype(F32)) * t).astype(o_ref.dtype)


def _aproj(attn, px, ga_blk, w_pa):
    b, t, d = attn.shape
    tm = _tile(t, 512)
    est = 2 * d * d * 2 + 2 * 3 * tm * d * 2 + 3 * tm * d * 4
    return pl.pallas_call(
        _aproj_kernel,
        out_shape=jax.ShapeDtypeStruct((b, t, d), BF16),
        grid=(b, t // tm),
        in_specs=[
            pl.BlockSpec((1, tm, d), lambda bi, i: (bi, i, 0)),
            pl.BlockSpec((1, tm, d), lambda bi, i: (bi, i, ga_blk)),
            pl.BlockSpec((d, d), lambda bi, i: (0, 0)),
        ],
        out_specs=pl.BlockSpec((1, tm, d), lambda bi, i: (bi, i, 0)),
        compiler_params=_params(("parallel", "parallel"), est),
        name="attn_proj",
    )(attn, px, w_pa)


HALO_ROWS = 16


def _bproj_kernel(cin_ref, cout_ref, cx_ref, cin_p_ref, cx_p_ref, cin_n_ref, cx_n_ref,
                  gb_ref, t1_ref, wc_ref, w_ref, o_ref):
    i, n = pl.program_id(1), pl.num_programs(1)
    tm = cin_ref.shape[1]
    z = cin_ref[0].astype(F32) * cx_ref[0].astype(F32)
    z_prev_row = cin_p_ref[0, HALO_ROWS - 1:HALO_ROWS, :].astype(F32) * cx_p_ref[0, HALO_ROWS - 1:HALO_ROWS, :].astype(F32)
    z_next_row = cin_n_ref[0, 0:1, :].astype(F32) * cx_n_ref[0, 0:1, :].astype(F32)
    z_prev_row = jnp.where(i > 0, z_prev_row, 0.0)
    z_next_row = jnp.where(i < n - 1, z_next_row, 0.0)
    row = lax.broadcasted_iota(jnp.int32, z.shape, 0)
    z_m1 = jnp.where(row == 0, z_prev_row, pltpu.roll(z, 1, 0))
    z_p1 = jnp.where(row == tm - 1, z_next_row, pltpu.roll(z, tm - 1, 0))
    conv = wc_ref[0:1, :] * z_m1 + wc_ref[1:2, :] * z + wc_ref[2:3, :] * z_p1
    cb = (cout_ref[0].astype(F32) * conv).astype(BF16)
    t = jnp.dot(cb, w_ref[...], preferred_element_type=F32)
    o_ref[0] = (t1_ref[0].astype(F32) + _sigmoid(gb_ref[0].astype(F32)) * t).astype(o_ref.dtype)


def _bproj(px, blks, t1, w_conv, w_pb):
    b, t, _ = px.shape
    d = t1.shape[2]
    tm = _tile(t, 512)
    hb = tm // HALO_ROWS
    n_h = t // HALO_ROWS
    cin_b, cout_b, cx_b, gb_b = blks

    def main(blk):
        return pl.BlockSpec((1, tm, d), lambda bi, i: (bi, i, blk))

    def prev(blk):
        return pl.BlockSpec((1, HALO_ROWS, d), lambda bi, i: (bi, jnp.maximum(i * hb - 1, 0), blk))

    def nxt(blk):
        return pl.BlockSpec((1, HALO_ROWS, d), lambda bi, i: (bi, jnp.minimum((i + 1) * hb, n_h - 1), blk))

    est = 2 * d * d * 2 + 2 * 6 * tm * d * 2 + 6 * tm * d * 4
    return pl.pallas_call(
        _bproj_kernel,
        out_shape=jax.ShapeDtypeStruct((b, t, d), BF16),
        grid=(b, t // tm),
        in_specs=[main(cin_b), main(cout_b), main(cx_b), prev(cin_b), prev(cx_b), nxt(cin_b), nxt(cx_b),
                  main(gb_b),
                  pl.BlockSpec((1, tm, d), lambda bi, i: (bi, i, 0)),
                  pl.BlockSpec((3, d), lambda bi, i: (0, 0)),
                  pl.BlockSpec((d, d), lambda bi, i: (0, 0))],
        out_specs=pl.BlockSpec((1, tm, d), lambda bi, i: (bi, i, 0)),
        compiler_params=_params(("parallel", "parallel"), est),
        name="conv_proj",
    )(px, px, px, px, px, px, px, px, t1, w_conv, w_pb)


def _oproj_kernel(m_ref, w_ref, x_ref, gt_ref, g_ref, o_ref):
    mix = jnp.dot(m_ref[0], w_ref[...], preferred_element_type=F32)
    o_ref[0] = x_ref[0] + gt_ref[0] * _rms(mix, g_ref[...])


def _oproj(m, w_o, x, gate, g_post):
    b, t, d = x.shape
    tm = _tile(t, 512)
    per_b = gate.shape[0] > 1
    gmap = (lambda bi, i: (bi, 0, 0)) if per_b else (lambda bi, i: (0, 0, 0))
    est = 2 * d * d * 2 + 2 * tm * d * (2 + 4 + 4) + 3 * tm * d * 4
    return pl.pallas_call(
        _oproj_kernel,
        out_shape=jax.ShapeDtypeStruct((b, t, d), F32),
        grid=(b, t // tm),
        in_specs=[
            pl.BlockSpec((1, tm, d), lambda bi, i: (bi, i, 0)),
            pl.BlockSpec((d, d), lambda bi, i: (0, 0)),
            pl.BlockSpec((1, tm, d), lambda bi, i: (bi, i, 0)),
            pl.BlockSpec((1, 1, d), gmap),
            pl.BlockSpec((1, d), lambda bi, i: (0, 0)),
        ],
        out_specs=pl.BlockSpec((1, tm, d), lambda bi, i: (bi, i, 0)),
        compiler_params=_params(("parallel", "parallel"), est),
        name="out_proj",
    )(m, w_o, x, gate, g_post.reshape(1, d))


def _swiglu_step(h_ref, wg_ref, wu_ref, wd_ref, acc_ref):
    h = h_ref[...]
    g = jnp.dot(h, wg_ref[...], preferred_element_type=F32)
    u = jnp.dot(h, wu_ref[...], preferred_element_type=F32)
    a = (g * _sigmoid(g) * u).astype(BF16)
    acc_ref[...] += jnp.dot(a, wd_ref[...], preferred_element_type=F32)


def _ffn_kernel(x_ref, g1_ref, sh_ref, sc_ref, gt_ref, g2_ref, wg_ref, wu_ref, wd_ref, o_ref, h_ref, acc_ref):
    j, nj = pl.program_id(2), pl.num_programs(2)

    @pl.when(j == 0)
    def _():
        _prenorm_to(h_ref, x_ref, g1_ref, sh_ref, sc_ref)
        acc_ref[...] = jnp.zeros_like(acc_ref)

    _swiglu_step(h_ref, wg_ref, wu_ref, wd_ref, acc_ref)

    @pl.when(j == nj - 1)
    def _():
        o_ref[0] = x_ref[0] + gt_ref[0] * _rms(acc_ref[...], g2_ref[...])


def _ffn_dense(x, g_pre, shift, scale, gate, g_post, wg, wu, wd):
    b, t, d = x.shape
    f = wg.shape[1]
    tm = _tile(t, 512)
    tf = _tile(f, 512)
    per_b = gate.shape[0] > 1
    mmap = (lambda bi, i, j: (bi, 0, 0)) if per_b else (lambda bi, i, j: (0, 0, 0))
    vec = pl.BlockSpec((1, d), lambda bi, i, j: (0, 0))
    mod = pl.BlockSpec((1, 1, d), mmap)
    est = 4 * tm * d * 4 + tm * d * 2 + tm * d * 4 + 2 * 3 * d * tf * 2 + 4 * tm * tf * 4 + 2 * tm * d * 4
    return pl.pallas_call(
        _ffn_kernel,
        out_shape=jax.ShapeDtypeStruct((b, t, d), F32),
        grid=(b, t // tm, f // tf),
        in_specs=[
            pl.BlockSpec((1, tm, d), lambda bi, i, j: (bi, i, 0)),
            vec, mod, mod, mod, vec,
            pl.BlockSpec((d, tf), lambda bi, i, j: (0, j)),
            pl.BlockSpec((d, tf), lambda bi, i, j: (0, j)),
            pl.BlockSpec((tf, d), lambda bi, i, j: (j, 0)),
        ],
        out_specs=pl.BlockSpec((1, tm, d), lambda bi, i, j: (bi, i, 0)),
        scratch_shapes=[pltpu.VMEM((tm, d), BF16), pltpu.VMEM((tm, d), F32)],
        compiler_params=_params(("parallel", "parallel", "arbitrary"), est),
        name="ffn_dense",
    )(x, g_pre.reshape(1, d), shift, scale, gate, g_post.reshape(1, d), wg, wu, wd)


def _router_kernel(x_ref, g_ref, sh_ref, sc_ref, wr_ref, h_ref, w_ref, i_ref):
    y = _rms(x_ref[0], g_ref[...])
    h = y * (1.0 + sc_ref[0]) + sh_ref[0]
    h_ref[...] = h
    logits = jnp.dot(h, wr_ref[...], preferred_element_type=F32, precision=lax.Precision.HIGHEST)
    lane = lax.broadcasted_iota(jnp.int32, logits.shape, 1)
    lane_f = lane.astype(F32)
    neg = jnp.float32(-jnp.inf)
    big = jnp.float32(V7X_LANES)
    lg = jnp.where(lane < N_EXPERTS, logits, neg)
    v1 = jnp.max(lg, axis=-1, keepdims=True)
    i1 = jnp.min(jnp.where(lg == v1, lane_f, big), axis=-1, keepdims=True)
    lg2 = jnp.where(lane_f == i1, neg, lg)
    v2 = jnp.max(lg2, axis=-1, keepdims=True)
    i2 = jnp.min(jnp.where(lg2 == v2, lane_f, big), axis=-1, keepdims=True)
    e = jnp.exp(v2 - v1)
    den = 1.0 + e
    w_ref[...] = jnp.where(lane == 0, 1.0 / den, jnp.where(lane == 1, e / den, 0.0))
    i_ref[...] = jnp.where(lane == 0, i1, jnp.where(lane == 1, i2, 0.0)).astype(jnp.int32)


def _router(x, g_pre, shift, scale, w_router_pad):
    b, t, d = x.shape
    tm = _tile(t, 512)
    nt = t // tm
    est = 2 * tm * d * 4 * 2 + 2 * d * V7X_LANES * 4 + 4 * tm * d * 4
    vec = pl.BlockSpec((1, d), lambda bi, i: (0, 0))
    mod = pl.BlockSpec((1, 1, d), lambda bi, i: (bi, 0, 0))
    row = lambda w: pl.BlockSpec((tm, w), lambda bi, i: (bi * nt + i, 0))
    return pl.pallas_call(
        _router_kernel,
        out_shape=(jax.ShapeDtypeStruct((b * t, d), F32),
                   jax.ShapeDtypeStruct((b * t, V7X_LANES), F32),
                   jax.ShapeDtypeStruct((b * t, V7X_LANES), jnp.int32)),
        grid=(b, nt),
        in_specs=[pl.BlockSpec((1, tm, d), lambda bi, i: (bi, i, 0)), vec, mod, mod,
                  pl.BlockSpec((d, V7X_LANES), lambda bi, i: (0, 0))],
        out_specs=(row(d), row(V7X_LANES), row(V7X_LANES)),
        compiler_params=_params(("parallel", "parallel"), est),
        name="moe_router",
    )(x, g_pre.reshape(1, d), shift, scale, w_router_pad)


DMA_LOOP_UNROLL = 8


def _row_copy(src, src_row, dst, dst_row, sem):
    return pltpu.make_async_copy(src.at[pl.ds(src_row, 1)], dst.at[pl.ds(dst_row, 1)], sem)


def _dispatch_kernel(p1_ref, p2_ref, h_ref, xs_in_ref, xs_ref, sem):
    del xs_in_ref
    tm = h_ref.shape[0]
    base = pl.program_id(0) * tm

    def start(r, c):
        _row_copy(h_ref, r, xs_ref, p1_ref[base + r], sem).start()
        _row_copy(h_ref, r, xs_ref, p2_ref[base + r], sem).start()
        return c

    def wait(r, c):
        _row_copy(h_ref, 0, xs_ref, 0, sem).wait()
        _row_copy(h_ref, 0, xs_ref, 0, sem).wait()
        return c

    lax.fori_loop(0, tm, start, 0, unroll=DMA_LOOP_UNROLL)
    lax.fori_loop(0, tm, wait, 0, unroll=DMA_LOOP_UNROLL)


def _dispatch(h, p1, p2, n_rows):
    n, d = h.shape
    tm = _tile(n, 512)
    xs0 = jnp.zeros((n_rows, d), F32)
    return pl.pallas_call(
        _dispatch_kernel,
        out_shape=jax.ShapeDtypeStruct((n_rows, d), F32),
        grid_spec=pltpu.PrefetchScalarGridSpec(
            num_scalar_prefetch=2,
            grid=(n // tm,),
            in_specs=[pl.BlockSpec((tm, d), lambda i, p1, p2: (i, 0)),
                      pl.BlockSpec(memory_space=pl.ANY)],
            out_specs=pl.BlockSpec(memory_space=pl.ANY),
            scratch_shapes=[pltpu.SemaphoreType.DMA],
        ),
        input_output_aliases={3: 0},
        compiler_params=_params(("arbitrary",), 2 * tm * d * 4),
        name="moe_dispatch",
    )(p1, p2, h, xs0)


def _moe_kernel(te_ref, tv_ref, xs_ref, wg_ref, wu_ref, wd_ref, ys_ref, h_ref, acc_ref):
    i, j, nj = pl.program_id(0), pl.program_id(1), pl.num_programs(1)
    valid = tv_ref[i] > 0

    @pl.when(jnp.logical_and(valid, j == 0))
    def _():
        h_ref[...] = xs_ref[...].astype(BF16)
        acc_ref[...] = jnp.zeros_like(acc_ref)

    @pl.when(valid)
    def _():
        _swiglu_step(h_ref, wg_ref.at[0], wu_ref.at[0], wd_ref.at[0], acc_ref)

    @pl.when(j == nj - 1)
    def _():
        @pl.when(valid)
        def _():
            ys_ref[...] = acc_ref[...]

        @pl.when(jnp.logical_not(valid))
        def _():
            ys_ref[...] = jnp.zeros_like(ys_ref)


def _moe_experts(xs, tile_e, tile_v, wg, wu, wd, tm):
    r, d = xs.shape
    f = wg.shape[2]
    tf = _tile(f, 256)
    nf = f // tf

    def fblk(i, j, te, tv):
        return jnp.where(tv[i] > 0, j, nf - 1)

    est = 2 * tm * d * 4 + tm * d * 2 + tm * d * 4 + 2 * 3 * d * tf * 2 + 4 * tm * tf * 4 + 2 * tm * d * 4
    return pl.pallas_call(
        _moe_kernel,
        out_shape=jax.ShapeDtypeStruct((r, d), F32),
        grid_spec=pltpu.PrefetchScalarGridSpec(
            num_scalar_prefetch=2,
            grid=(r // tm, nf),
            in_specs=[pl.BlockSpec((tm, d), lambda i, j, te, tv: (i, 0)),
                      pl.BlockSpec((1, d, tf), lambda i, j, te, tv: (te[i], 0, fblk(i, j, te, tv))),
                      pl.BlockSpec((1, d, tf), lambda i, j, te, tv: (te[i], 0, fblk(i, j, te, tv))),
                      pl.BlockSpec((1, tf, d), lambda i, j, te, tv: (te[i], fblk(i, j, te, tv), 0))],
            out_specs=pl.BlockSpec((tm, d), lambda i, j, te, tv: (i, 0)),
            scratch_shapes=[pltpu.VMEM((tm, d), BF16), pltpu.VMEM((tm, d), F32)],
        ),
        compiler_params=_params(("arbitrary", "arbitrary"), est),
        name="moe_experts",
    )(tile_e, tile_v, xs, wg, wu, wd)


def _combine_kernel(p1_ref, p2_ref, ys_ref, x_ref, w_ref, gt_ref, g_ref, o_ref, ybuf, sem):
    i, n = pl.program_id(0), pl.num_programs(0)
    tm = x_ref.shape[0]

    def issue(step, slot):
        base = step * tm

        def body(r, c):
            _row_copy(ys_ref, p1_ref[base + r], ybuf.at[slot, 0], r, sem.at[slot]).start()
            _row_copy(ys_ref, p2_ref[base + r], ybuf.at[slot, 1], r, sem.at[slot]).start()
            return c

        lax.fori_loop(0, tm, body, 0, unroll=DMA_LOOP_UNROLL)

    @pl.when(i == 0)
    def _():
        issue(0, 0)

    @pl.when(i + 1 < n)
    def _():
        issue(i + 1, (i + 1) % 2)

    slot = i % 2

    def wait(r, c):
        _row_copy(ys_ref, 0, ybuf.at[slot, 0], 0, sem.at[slot]).wait()
        _row_copy(ys_ref, 0, ybuf.at[slot, 1], 0, sem.at[slot]).wait()
        return c

    lax.fori_loop(0, tm, wait, 0, unroll=DMA_LOOP_UNROLL)
    w = w_ref[...]
    y = w[:, 0:1] * ybuf[slot, 0] + w[:, 1:2] * ybuf[slot, 1]
    o_ref[...] = x_ref[...] + gt_ref[0] * _rms(y, g_ref[...])


def _combine(ys, p1, p2, x2d, topw, gate, g_post, tokens_per_batch):
    n, d = x2d.shape
    tm = _tile(tokens_per_batch, 512)
    nt = tokens_per_batch // tm
    est = 4 * tm * d * 4 + 4 * tm * d * 4 + 3 * tm * d * 4
    return pl.pallas_call(
        _combine_kernel,
        out_shape=jax.ShapeDtypeStruct((n, d), F32),
        grid_spec=pltpu.PrefetchScalarGridSpec(
            num_scalar_prefetch=2,
            grid=(n // tm,),
            in_specs=[pl.BlockSpec(memory_space=pl.ANY),
                      pl.BlockSpec((tm, d), lambda i, p1, p2: (i, 0)),
                      pl.BlockSpec((tm, V7X_LANES), lambda i, p1, p2: (i, 0)),
                      pl.BlockSpec((1, 1, d), lambda i, p1, p2: (i // nt, 0, 0)),
                      pl.BlockSpec((1, d), lambda i, p1, p2: (0, 0))],
            out_specs=pl.BlockSpec((tm, d), lambda i, p1, p2: (i, 0)),
            scratch_shapes=[pltpu.VMEM((2, 2, tm, d), F32), pltpu.SemaphoreType.DMA((2,))],
        ),
        compiler_params=_params(("arbitrary",), est),
        name="moe_combine",
    )(p1, p2, ys, x2d, topw, gate, g_post.reshape(1, d))


def _route_plan(e1, e2, tm, n_tiles):
    ar = jnp.arange(N_EXPERTS, dtype=jnp.int32)
    cnt = (e1[:, None] == ar).astype(jnp.int32) + (e2[:, None] == ar).astype(jnp.int32)
    csum = jnp.cumsum(cnt, axis=0)
    padded = ((csum[-1] + tm - 1) // tm) * tm
    ends = jnp.cumsum(padded)
    pos = (ends - padded)[None, :] + csum - 1
    p1 = jnp.take_along_axis(pos, e1[:, None], axis=1)[:, 0]
    p2 = jnp.take_along_axis(pos, e2[:, None], axis=1)[:, 0]
    starts = jnp.arange(n_tiles, dtype=jnp.int32) * tm
    tile_e = jnp.sum((starts[:, None] >= ends[None, :]).astype(jnp.int32), axis=1)
    tile_v = (starts < ends[-1]).astype(jnp.int32)
    last_e = jnp.max(jnp.where(tile_v > 0, tile_e, 0))
    tile_e = jnp.where(tile_v > 0, tile_e, last_e)
    return p1.astype(jnp.int32), p2.astype(jnp.int32), tile_e.astype(jnp.int32), tile_v


def _ffn_moe(x, g_pre, shift, scale, gate, g_post, w_router, wg, wu, wd):
    b, t, d = x.shape
    n = b * t
    wr = jnp.zeros((d, V7X_LANES), F32).at[:, :N_EXPERTS].set(w_router)
    h, topw, topi = _router(x, g_pre, shift, scale, wr)
    tm = _tile(n, 1024)
    n_tiles = (2 * n) // tm + N_EXPERTS
    p1, p2, tile_e, tile_v = _route_plan(topi[:, 0], topi[:, 1], tm, n_tiles)
    xs = _dispatch(h, p1, p2, n_tiles * tm)
    ys = _moe_experts(xs, tile_e, tile_v, wg, wu, wd, tm)
    out = _combine(ys, p1, p2, x.reshape(n, d), topw, gate, g_post, t)
    return out.reshape(b, t, d)


def _deinterleave_heads(w):
    lead, n = w.shape[:-1], w.shape[-1] // HEAD_DIM
    w = w.reshape(*lead, n, HEAD_DIM // 2, 2)
    return jnp.swapaxes(w, -1, -2).reshape(*lead, n * HEAD_DIM)


def _permute_w_in(w_in):
    d = w_in.shape[1]
    kv = d // GROUP
    q, k = _deinterleave_heads(w_in[..., :d]), _deinterleave_heads(w_in[..., d:d + kv])
    return jnp.concatenate([q, w_in[..., d + 2 * kv:], k, w_in[..., d + kv:d + 2 * kv]], axis=-1).astype(BF16)


def _rope_tables(t):
    rows = t // GRID_W
    row = jnp.repeat(jnp.arange(rows), GRID_W).astype(F32)
    col = jnp.tile(jnp.arange(GRID_W), rows).astype(F32)
    n_pairs_axis = HEAD_DIM // 4
    inv = ROPE_THETA ** (-jnp.arange(n_pairs_axis, dtype=F32) / n_pairs_axis)
    ang = jnp.concatenate([row[:, None] * inv[None, :], col[:, None] * inv[None, :]], axis=-1)
    cos, sin = jnp.cos(ang), jnp.sin(ang)
    return jnp.concatenate([cos, cos], axis=-1), jnp.concatenate([-sin, sin], axis=-1)


def _mixer(tok, mod, l, last_unused, w_in_l, gq, gk, w_conv_l, w_pa_l, w_pb_l, w_o_l, g_pre, g_post,
           cc, ss, rope, extra_k=None, extra_v=None):
    del l, last_unused
    b, t, d = tok.shape
    n_heads, n_kv = d // HEAD_DIM, d // HEAD_DIM // GROUP
    kv = n_kv * HEAD_DIM
    sh1, sc1, gt1 = mod
    px = _inproj(tok, g_pre, sh1, sc1, w_in_l)
    q_r, k_r = _qk_norm_rope(px, 0, px, 6 * d // kv, n_heads, n_kv, gq, gk, cc, ss, rope)
    k_all, v_all = k_r, px[:, :, 6 * d + kv:]
    if extra_k is not None:
        k_all = jnp.concatenate([extra_k, k_r], axis=2)
        v_all = jnp.concatenate([extra_v, v_all], axis=1)
    attn = _attention(q_r, jnp.swapaxes(k_all, 2, 3), _v_ext(v_all, n_kv))
    t1 = _aproj(attn, px, 4, w_pa_l)
    m = _bproj(px, (1, 2, 3, 5), t1, w_conv_l, w_pb_l)
    out = _oproj(m, w_o_l, tok, gt1, g_post)
    return out, k_r, px


def kernel(x, c, ctx, c_ctx, w_ada, b_ada, g_pre_mix, g_post_mix, g_pre_ffn, g_post_ffn, w_in, g_q, g_k,
           w_conv, w_pa, w_pb, w_o, w_dense_gate, w_dense_up, w_dense_down, w_router, w_moe_gate, w_moe_up,
           w_moe_down):
    b, t, d = x.shape
    c_len = ctx.shape[1]
    depth = w_in.shape[0]
    n_kv = d // HEAD_DIM // GROUP
    kv = n_kv * HEAD_DIM

    w_in_p = _permute_w_in(w_in)
    gq_p, gk_p = _deinterleave_heads(g_q), _deinterleave_heads(g_k)
    cc, ss = _rope_tables(t)
    cc_c, ss_c = cc[:c_len], ss[:c_len]

    n_rows = -(-(b + 1) // 16) * 16
    c_rows = jnp.zeros((n_rows, d), F32).at[:b].set(c).at[b].set(c_ctx)
    mods = _ada(c_rows, w_ada, b_ada)

    for l in range(depth):
        last = l == depth - 1
        mx = [mods[l, :b, i * d:(i + 1) * d].reshape(b, 1, d) for i in range(6)]
        mc = [mods[l, b:b + 1, i * d:(i + 1) * d].reshape(1, 1, d) for i in range(6)]
        wpa, wpb, wo = w_pa[l].astype(BF16), w_pb[l].astype(BF16), w_o[l].astype(BF16)

        if last:
            pc_kv = _inproj(ctx, g_pre_mix[l], mc[0], mc[1], w_in_p[l][:, 6 * d:])
            _, kc = _qk_norm_rope(pc_kv, 0, pc_kv, 0, n_kv, n_kv, gk_p[l], gk_p[l], cc_c, ss_c, False)
            vc = pc_kv[:, :, kv:]
            ctx_new = None
        else:
            ctx_new, kc, pc = _mixer(ctx, (mc[0], mc[1], mc[2]), l, None, w_in_p[l], gq_p[l], gk_p[l], w_conv[l],
                                     wpa, wpb, wo, g_pre_mix[l], g_post_mix[l], cc_c, ss_c, False)
            vc = pc[:, :, 6 * d + kv:]
        x, _, _ = _mixer(x, (mx[0], mx[1], mx[2]), l, None, w_in_p[l], gq_p[l], gk_p[l], w_conv[l],
                         wpa, wpb, wo, g_pre_mix[l], g_post_mix[l], cc, ss, True, extra_k=kc, extra_v=vc)
        if not last:
            ctx = ctx_new

        i = l // 2
        if l % 2 == 0:
            wg, wu, wd = (w_dense_gate[i].astype(BF16), w_dense_up[i].astype(BF16), w_dense_down[i].astype(BF16))
            x = _ffn_dense(x, g_pre_ffn[l], mx[3], mx[4], mx[5], g_post_ffn[l], wg, wu, wd)
            if not last:
                ctx = _ffn_dense(ctx, g_pre_ffn[l], mc[3], mc[4], mc[5], g_post_ffn[l], wg, wu, wd)
        else:
            wg, wu, wd = (w_moe_gate[i].astype(BF16), w_moe_up[i].astype(BF16), w_moe_down[i].astype(BF16))
            x = _ffn_moe(x, g_pre_ffn[l], mx[3], mx[4], mx[5], g_post_ffn[l], w_router[i], wg, wu, wd)
            if not last:
                ctx = _ffn_moe(ctx, g_pre_ffn[l], jnp.broadcast_to(mc[3], (b, 1, d)), jnp.broadcast_to(mc[4], (b, 1, d)),
                               jnp.broadcast_to(mc[5], (b, 1, d)), g_post_ffn[l], w_router[i], wg, wu, wd)
    return x
```

```python
import functools

import jax
import jax.numpy as jnp
from jax import lax
from jax.experimental import pallas as pl
from jax.experimental.pallas import tpu as pltpu

HEAD_DIM = 128
GROUP = 4
GRID_W = 64
N_EXPERTS = 8
EPS = 1e-6
ROPE_THETA = 10000.0
LOG2_E = 1.4426950408889634

V7X_LANES = 128
V7X_VMEM_BYTES = 64 * 1024 * 1024
VMEM_RESERVE_BYTES = 8 * 1024 * 1024

F32 = jnp.float32
BF16 = jnp.bfloat16


def _params(semantics, est_bytes):
    limit = int(min(V7X_VMEM_BYTES - VMEM_RESERVE_BYTES, max(32 * 1024 * 1024, 2 * est_bytes)))
    return pltpu.CompilerParams(dimension_semantics=semantics, vmem_limit_bytes=limit)


def _tile(n, pref):
    t = min(n, pref)
    while n % t:
        t //= 2
    return t


def _rms(x32, g):
    ms = jnp.mean(x32 * x32, axis=-1, keepdims=True)
    return x32 * lax.rsqrt(ms + EPS) * g


def _sigmoid(x):
    return 1.0 / (1.0 + jnp.exp(-x))


NORM_CHUNK_ROWS = 256


def _prenorm_to(h_ref, x_ref, g_ref, sh_ref, sc_ref):
    tm = h_ref.shape[0]
    rows = min(tm, NORM_CHUNK_ROWS)

    def body(r, c):
        sl = pl.ds(pl.multiple_of(r * rows, rows), rows)
        y = _rms(x_ref[0, sl, :], g_ref[...])
        h_ref[sl, :] = (y * (1.0 + sc_ref[0]) + sh_ref[0]).astype(h_ref.dtype)
        return c

    lax.fori_loop(0, tm // rows, body, 0)


def _ada_kernel(c_ref, w_ref, b_ref, o_ref):
    c = c_ref[...]
    s = (c * _sigmoid(c)).astype(BF16)
    w = w_ref[0].astype(BF16)
    o_ref[0] = jnp.dot(s, w, preferred_element_type=F32) + b_ref[0]


def _ada(c_rows, w_ada, b_ada):
    depth, d, n = w_ada.shape
    r = c_rows.shape[0]
    tn = _tile(n, 1024)
    est = 2 * d * tn * 4 + 2 * r * tn * 4 + r * d * 4
    return pl.pallas_call(
        _ada_kernel,
        out_shape=jax.ShapeDtypeStruct((depth, r, n), F32),
        grid=(depth, n // tn),
        in_specs=[
            pl.BlockSpec((r, d), lambda l, j: (0, 0)),
            pl.BlockSpec((1, d, tn), lambda l, j: (l, 0, j)),
            pl.BlockSpec((1, 1, tn), lambda l, j: (l, 0, j)),
        ],
        out_specs=pl.BlockSpec((1, r, tn), lambda l, j: (l, 0, j)),
        compiler_params=_params(("parallel", "parallel"), est),
        name="ada_mod",
    )(c_rows, w_ada, b_ada.reshape(depth, 1, n))


def _inproj_kernel(x_ref, g_ref, sh_ref, sc_ref, w_ref, o_ref, h_ref):
    @pl.when(pl.program_id(2) == 0)
    def _():
        _prenorm_to(h_ref, x_ref, g_ref, sh_ref, sc_ref)

    o_ref[0] = jnp.dot(h_ref[...], w_ref[...], preferred_element_type=F32).astype(o_ref.dtype)


def _inproj(x, g, shift, scale, w):
    b, t, d = x.shape
    n = w.shape[1]
    tm = _tile(t, 1024)
    tn = _tile(n, 1024)
    per_b = shift.shape[0] > 1
    mod_map = (lambda bi, i, j: (bi, 0, 0)) if per_b else (lambda bi, i, j: (0, 0, 0))
    est = 2 * tm * d * 4 + tm * d * 2 + 2 * d * tn * 2 + 2 * tm * tn * 2 + 3 * tm * d * 4
    return pl.pallas_call(
        _inproj_kernel,
        out_shape=jax.ShapeDtypeStruct((b, t, n), BF16),
        grid=(b, t // tm, n // tn),
        in_specs=[
            pl.BlockSpec((1, tm, d), lambda bi, i, j: (bi, i, 0)),
            pl.BlockSpec((1, d), lambda bi, i, j: (0, 0)),
            pl.BlockSpec((1, 1, d), mod_map),
            pl.BlockSpec((1, 1, d), mod_map),
            pl.BlockSpec((d, tn), lambda bi, i, j: (0, j)),
        ],
        out_specs=pl.BlockSpec((1, tm, tn), lambda bi, i, j: (bi, i, j)),
        scratch_shapes=[pltpu.VMEM((tm, d), BF16)],
        compiler_params=_params(("parallel", "parallel", "arbitrary"), est),
        name="inproj",
    )(x, g.reshape(1, d), shift, scale, w)


def _qk_kernel(q_ref, k_ref, gq_ref, gk_ref, cc_ref, ss_ref, qo_ref, ko_ref, *, rope, q_scale):
    def head(t, g, scale):
        y = _rms(t.astype(F32), g)
        if rope:
            y = y * cc_ref[...] + pltpu.roll(y, HEAD_DIM // 2, 1) * ss_ref[...]
        if scale != 1.0:
            y = y * scale
        return y.astype(BF16)

    for hh in range(qo_ref.shape[1]):
        qo_ref[0, hh] = head(q_ref[0, :, hh * HEAD_DIM:(hh + 1) * HEAD_DIM], gq_ref[...], q_scale)
    for hh in range(ko_ref.shape[1]):
        ko_ref[0, hh] = head(k_ref[0, :, hh * HEAD_DIM:(hh + 1) * HEAD_DIM], gk_ref[...], 1.0)


def _qk_norm_rope(pq, q_blk, pk, k_blk, n_heads, n_kv, gq, gk, cc, ss, rope):
    b, t = pq.shape[:2]
    tm = _tile(t, 512)
    qw, kw = n_heads * HEAD_DIM, n_kv * HEAD_DIM
    est = 2 * tm * (qw + kw) * 2 * 2 + 6 * tm * HEAD_DIM * 4
    kern = functools.partial(_qk_kernel, rope=rope, q_scale=LOG2_E * HEAD_DIM ** -0.5)
    return pl.pallas_call(
        kern,
        out_shape=(jax.ShapeDtypeStruct((b, n_heads, t, HEAD_DIM), BF16),
                   jax.ShapeDtypeStruct((b, n_kv, t, HEAD_DIM), BF16)),
        grid=(b, t // tm),
        in_specs=[
            pl.BlockSpec((1, tm, qw), lambda bi, i: (bi, i, q_blk)),
            pl.BlockSpec((1, tm, kw), lambda bi, i: (bi, i, k_blk)),
            pl.BlockSpec((1, HEAD_DIM), lambda bi, i: (0, 0)),
            pl.BlockSpec((1, HEAD_DIM), lambda bi, i: (0, 0)),
            pl.BlockSpec((tm, HEAD_DIM), lambda bi, i: (i, 0)),
            pl.BlockSpec((tm, HEAD_DIM), lambda bi, i: (i, 0)),
        ],
        out_specs=(pl.BlockSpec((1, n_heads, tm, HEAD_DIM), lambda bi, i: (bi, 0, i, 0)),
                   pl.BlockSpec((1, n_kv, tm, HEAD_DIM), lambda bi, i: (bi, 0, i, 0))),
        compiler_params=_params(("parallel", "parallel"), est),
        name="qk_norm_rope" if rope else "qk_norm",
    )(pq, pk, gq.reshape(1, HEAD_DIM), gk.reshape(1, HEAD_DIM), cc, ss)


def _attn_kernel(q_ref, kt_ref, v_ref, o_ref):
    kt, v = kt_ref[0, 0], v_ref[0, 0]
    for gi in range(q_ref.shape[1]):
        s = jnp.dot(q_ref[0, gi], kt, preferred_element_type=F32)
        m = jnp.max(s, axis=-1, keepdims=True)
        p = jnp.exp2(s - m).astype(BF16)
        o = jnp.dot(p, v, preferred_element_type=F32)
        o_ref[0, :, gi * HEAD_DIM:(gi + 1) * HEAD_DIM] = (
            o[:, :HEAD_DIM] / o[:, HEAD_DIM:HEAD_DIM + 1]).astype(o_ref.dtype)


def _attention(q, kt, v_ext):
    b, h, t, _ = q.shape
    n_kv, l = kt.shape[1], kt.shape[3]
    tq = _tile(t, 512)
    gw = GROUP * HEAD_DIM
    est = 2 * (GROUP * tq * HEAD_DIM * 2 + 3 * HEAD_DIM * l * 2 + tq * gw * 2) + GROUP * tq * l * 8
    return pl.pallas_call(
        _attn_kernel,
        out_shape=jax.ShapeDtypeStruct((b, t, h * HEAD_DIM), BF16),
        grid=(b, n_kv, t // tq),
        in_specs=[
            pl.BlockSpec((1, GROUP, tq, HEAD_DIM), lambda bi, kv, i: (bi, kv, i, 0)),
            pl.BlockSpec((1, 1, HEAD_DIM, l), lambda bi, kv, i: (bi, kv, 0, 0)),
            pl.BlockSpec((1, 1, l, 2 * HEAD_DIM), lambda bi, kv, i: (bi, kv, 0, 0)),
        ],
        out_specs=pl.BlockSpec((1, tq, gw), lambda bi, kv, i: (bi, i, kv)),
        compiler_params=_params(("parallel", "parallel", "parallel"), est),
        name="gqa_attention",
    )(q, kt, v_ext)


def _v_ext(v, n_kv):
    b, l, _ = v.shape
    vh = jnp.swapaxes(v.reshape(b, l, n_kv, HEAD_DIM), 1, 2)
    return jnp.concatenate([vh, jnp.ones_like(vh)], axis=-1)


def _aproj_kernel(a_ref, ga_ref, w_ref, o_ref):
    t = jnp.dot(a_ref[0], w_ref[...], preferred_element_type=F32)
    o_ref[0] = (_sigmoid(ga_ref[0].astype(F32)) * t).astype(o_ref.dtype)


def _aproj(attn, px, ga_blk, w_pa):
    b, t, d = attn.shape
    tm = _tile(t, 512)
    est = 2 * d * d * 2 + 2 * 3 * tm * d * 2 + 3 * tm * d * 4
    return pl.pallas_call(
        _aproj_kernel,
        out_shape=jax.ShapeDtypeStruct((b, t, d), BF16),
        grid=(b, t // tm),
        in_specs=[
            pl.BlockSpec((1, tm, d), lambda bi, i: (bi, i, 0)),
            pl.BlockSpec((1, tm, d), lambda bi, i: (bi, i, ga_blk)),
            pl.BlockSpec((d, d), lambda bi, i: (0, 0)),
        ],
        out_specs=pl.BlockSpec((1, tm, d), lambda bi, i: (bi, i, 0)),
        compiler_params=_params(("parallel", "parallel"), est),
        name="attn_proj",
    )(attn, px, w_pa)


HALO_ROWS = 16


def _bproj_kernel(cin_ref, cout_ref, cx_ref, cin_p_ref, cx_p_ref, cin_n_ref, cx_n_ref,
                  gb_ref, t1_ref, wc_ref, w_ref, o_ref):
    i, n = pl.program_id(1), pl.num_programs(1)
    tm = cin_ref.shape[1]
    z = cin_ref[0].astype(F32) * cx_ref[0].astype(F32)
    z_prev_row = cin_p_ref[0, HALO_ROWS - 1:HALO_ROWS, :].astype(F32) * cx_p_ref[0, HALO_ROWS - 1:HALO_ROWS, :].astype(F32)
    z_next_row = cin_n_ref[0, 0:1, :].astype(F32) * cx_n_ref[0, 0:1, :].astype(F32)
    z_prev_row = jnp.where(i > 0, z_prev_row, 0.0)
    z_next_row = jnp.where(i < n - 1, z_next_row, 0.0)
    row = lax.broadcasted_iota(jnp.int32, z.shape, 0)
    z_m1 = jnp.where(row == 0, z_prev_row, pltpu.roll(z, 1, 0))
    z_p1 = jnp.where(row == tm - 1, z_next_row, pltpu.roll(z, tm - 1, 0))
    conv = wc_ref[0:1, :] * z_m1 + wc_ref[1:2, :] * z + wc_ref[2:3, :] * z_p1
    cb = (cout_ref[0].astype(F32) * conv).astype(BF16)
    t = jnp.dot(cb, w_ref[...], preferred_element_type=F32)
    o_ref[0] = (t1_ref[0].astype(F32) + _sigmoid(gb_ref[0].astype(F32)) * t).astype(o_ref.dtype)


def _bproj(px, blks, t1, w_conv, w_pb):
    b, t, _ = px.shape
    d = t1.shape[2]
    tm = _tile(t, 512)
    hb = tm // HALO_ROWS
    n_h = t // HALO_ROWS
    cin_b, cout_b, cx_b, gb_b = blks

    def main(blk):
        return pl.BlockSpec((1, tm, d), lambda bi, i: (bi, i, blk))

    def prev(blk):
        return pl.BlockSpec((1, HALO_ROWS, d), lambda bi, i: (bi, jnp.maximum(i * hb - 1, 0), blk))

    def nxt(blk):
        return pl.BlockSpec((1, HALO_ROWS, d), lambda bi, i: (bi, jnp.minimum((i + 1) * hb, n_h - 1), blk))

    est = 2 * d * d * 2 + 2 * 6 * tm * d * 2 + 6 * tm * d * 4
    return pl.pallas_call(
        _bproj_kernel,
        out_shape=jax.ShapeDtypeStruct((b, t, d), BF16),
        grid=(b, t // tm),
        in_specs=[main(cin_b), main(cout_b), main(cx_b), prev(cin_b), prev(cx_b), nxt(cin_b), nxt(cx_b),
                  main(gb_b),
                  pl.BlockSpec((1, tm, d), lambda bi, i: (bi, i, 0)),
                  pl.BlockSpec((3, d), lambda bi, i: (0, 0)),
                  pl.BlockSpec((d, d), lambda bi, i: (0, 0))],
        out_specs=pl.BlockSpec((1, tm, d), lambda bi, i: (bi, i, 0)),
        compiler_params=_params(("parallel", "parallel"), est),
        name="conv_proj",
    )(px, px, px, px, px, px, px, px, t1, w_conv, w_pb)


def _oproj_kernel(m_ref, w_ref, x_ref, gt_ref, g_ref, o_ref):
    mix = jnp.dot(m_ref[0], w_ref[...], preferred_element_type=F32)
    o_ref[0] = x_ref[0] + gt_ref[0] * _rms(mix, g_ref[...])


def _oproj(m, w_o, x, gate, g_post):
    b, t, d = x.shape
    tm = _tile(t, 512)
    per_b = gate.shape[0] > 1
    gmap = (lambda bi, i: (bi, 0, 0)) if per_b else (lambda bi, i: (0, 0, 0))
    est = 2 * d * d * 2 + 2 * tm * d * (2 + 4 + 4) + 3 * tm * d * 4
    return pl.pallas_call(
        _oproj_kernel,
        out_shape=jax.ShapeDtypeStruct((b, t, d), F32),
        grid=(b, t // tm),
        in_specs=[
            pl.BlockSpec((1, tm, d), lambda bi, i: (bi, i, 0)),
            pl.BlockSpec((d, d), lambda bi, i: (0, 0)),
            pl.BlockSpec((1, tm, d), lambda bi, i: (bi, i, 0)),
            pl.BlockSpec((1, 1, d), gmap),
            pl.BlockSpec((1, d), lambda bi, i: (0, 0)),
        ],
        out_specs=pl.BlockSpec((1, tm, d), lambda bi, i: (bi, i, 0)),
        compiler_params=_params(("parallel", "parallel"), est),
        name="out_proj",
    )(m, w_o, x, gate, g_post.reshape(1, d))


def _swiglu_step(h_ref, wg_ref, wu_ref, wd_ref, acc_ref):
    h = h_ref[...]
    g = jnp.dot(h, wg_ref[...].astype(BF16), preferred_element_type=F32)
    u = jnp.dot(h, wu_ref[...].astype(BF16), preferred_element_type=F32)
    a = (g * _sigmoid(g) * u).astype(BF16)
    acc_ref[...] += jnp.dot(a, wd_ref[...].astype(BF16), preferred_element_type=F32)


def _ffn_kernel(x_ref, g1_ref, sh_ref, sc_ref, gt_ref, g2_ref, wg_ref, wu_ref, wd_ref, o_ref, h_ref, acc_ref):
    j, nj = pl.program_id(2), pl.num_programs(2)

    @pl.when(j == 0)
    def _():
        _prenorm_to(h_ref, x_ref, g1_ref, sh_ref, sc_ref)
        acc_ref[...] = jnp.zeros_like(acc_ref)

    _swiglu_step(h_ref, wg_ref, wu_ref, wd_ref, acc_ref)

    @pl.when(j == nj - 1)
    def _():
        o_ref[0] = x_ref[0] + gt_ref[0] * _rms(acc_ref[...], g2_ref[...])


def _ffn_dense(x, g_pre, shift, scale, gate, g_post, wg, wu, wd):
    b, t, d = x.shape
    f = wg.shape[1]
    tm = _tile(t, 512)
    tf = _tile(f, 1024)
    per_b = gate.shape[0] > 1
    mmap = (lambda bi, i, j: (bi, 0, 0)) if per_b else (lambda bi, i, j: (0, 0, 0))
    vec = pl.BlockSpec((1, d), lambda bi, i, j: (0, 0))
    mod = pl.BlockSpec((1, 1, d), mmap)
    est = 4 * tm * d * 4 + tm * d * 2 + tm * d * 4 + 2 * 3 * d * tf * 2 + 4 * tm * tf * 4 + 2 * tm * d * 4
    return pl.pallas_call(
        _ffn_kernel,
        out_shape=jax.ShapeDtypeStruct((b, t, d), F32),
        grid=(b, t // tm, f // tf),
        in_specs=[
            pl.BlockSpec((1, tm, d), lambda bi, i, j: (bi, i, 0)),
            vec, mod, mod, mod, vec,
            pl.BlockSpec((d, tf), lambda bi, i, j: (0, j)),
            pl.BlockSpec((d, tf), lambda bi, i, j: (0, j)),
            pl.BlockSpec((tf, d), lambda bi, i, j: (j, 0)),
        ],
        out_specs=pl.BlockSpec((1, tm, d), lambda bi, i, j: (bi, i, 0)),
        scratch_shapes=[pltpu.VMEM((tm, d), BF16), pltpu.VMEM((tm, d), F32)],
        compiler_params=_params(("parallel", "parallel", "arbitrary"), est),
        name="ffn_dense",
    )(x, g_pre.reshape(1, d), shift, scale, gate, g_post.reshape(1, d), wg, wu, wd)


HI16 = 0xFFFF0000


def _pack_rows(v):
    half = v.shape[1] // 2
    lo = pltpu.bitcast(v[:, :half].astype(BF16).astype(F32), jnp.uint32)
    hi = pltpu.bitcast(v[:, half:].astype(BF16).astype(F32), jnp.uint32)
    return (lo >> 16) | (hi & jnp.uint32(HI16))


def _unpack_rows(u):
    return pltpu.bitcast(u << 16, F32), pltpu.bitcast(u & jnp.uint32(HI16), F32)


def _router_kernel(x_ref, g_ref, sh_ref, sc_ref, wr_ref, h_ref, w_ref, i_ref):
    y = _rms(x_ref[0], g_ref[...])
    h = y * (1.0 + sc_ref[0]) + sh_ref[0]
    h_ref[...] = _pack_rows(h)
    logits = jnp.dot(h, wr_ref[...], preferred_element_type=F32, precision=lax.Precision.HIGHEST)
    lane = lax.broadcasted_iota(jnp.int32, logits.shape, 1)
    lane_f = lane.astype(F32)
    neg = jnp.float32(-jnp.inf)
    big = jnp.float32(V7X_LANES)
    lg = jnp.where(lane < N_EXPERTS, logits, neg)
    v1 = jnp.max(lg, axis=-1, keepdims=True)
    i1 = jnp.min(jnp.where(lg == v1, lane_f, big), axis=-1, keepdims=True)
    lg2 = jnp.where(lane_f == i1, neg, lg)
    v2 = jnp.max(lg2, axis=-1, keepdims=True)
    i2 = jnp.min(jnp.where(lg2 == v2, lane_f, big), axis=-1, keepdims=True)
    e = jnp.exp(v2 - v1)
    den = 1.0 + e
    w_ref[...] = jnp.where(lane == 0, 1.0 / den, jnp.where(lane == 1, e / den, 0.0))
    i_ref[...] = jnp.where(lane == 0, i1, jnp.where(lane == 1, i2, 0.0)).astype(jnp.int32)


def _router(x, g_pre, shift, scale, w_router_pad):
    b, t, d = x.shape
    tm = _tile(t, 512)
    nt = t // tm
    est = 2 * tm * d * 4 * 2 + 2 * d * V7X_LANES * 4 + 4 * tm * d * 4
    vec = pl.BlockSpec((1, d), lambda bi, i: (0, 0))
    mod = pl.BlockSpec((1, 1, d), lambda bi, i: (bi, 0, 0))
    row = lambda w: pl.BlockSpec((tm, w), lambda bi, i: (bi * nt + i, 0))
    return pl.pallas_call(
        _router_kernel,
        out_shape=(jax.ShapeDtypeStruct((b * t, d // 2), jnp.uint32),
                   jax.ShapeDtypeStruct((b * t, V7X_LANES), F32),
                   jax.ShapeDtypeStruct((b * t, V7X_LANES), jnp.int32)),
        grid=(b, nt),
        in_specs=[pl.BlockSpec((1, tm, d), lambda bi, i: (bi, i, 0)), vec, mod, mod,
                  pl.BlockSpec((d, V7X_LANES), lambda bi, i: (0, 0))],
        out_specs=(row(d // 2), row(V7X_LANES), row(V7X_LANES)),
        compiler_params=_params(("parallel", "parallel"), est),
        name="moe_router",
    )(x, g_pre.reshape(1, d), shift, scale, w_router_pad)


DMA_LOOP_UNROLL = 8


def _row_copy(src, src_row, dst, dst_row, sem):
    return pltpu.make_async_copy(src.at[pl.ds(src_row, 1)], dst.at[pl.ds(dst_row, 1)], sem)


def _dispatch_kernel(p1_ref, p2_ref, h_ref, xs_in_ref, xs_ref, sem):
    del xs_in_ref
    tm = h_ref.shape[0]
    base = pl.program_id(0) * tm

    def start(r, c):
        _row_copy(h_ref, r, xs_ref, p1_ref[base + r], sem).start()
        _row_copy(h_ref, r, xs_ref, p2_ref[base + r], sem).start()
        return c

    def wait(r, c):
        _row_copy(h_ref, 0, xs_ref, 0, sem).wait()
        _row_copy(h_ref, 0, xs_ref, 0, sem).wait()
        return c

    lax.fori_loop(0, tm, start, 0, unroll=DMA_LOOP_UNROLL)
    lax.fori_loop(0, tm, wait, 0, unroll=DMA_LOOP_UNROLL)


def _dispatch(h, p1, p2, n_rows):
    n, d = h.shape
    tm = _tile(n, 512)
    xs0 = jnp.zeros((n_rows, d), h.dtype)
    return pl.pallas_call(
        _dispatch_kernel,
        out_shape=jax.ShapeDtypeStruct((n_rows, d), h.dtype),
        grid_spec=pltpu.PrefetchScalarGridSpec(
            num_scalar_prefetch=2,
            grid=(n // tm,),
            in_specs=[pl.BlockSpec((tm, d), lambda i, p1, p2: (i, 0)),
                      pl.BlockSpec(memory_space=pl.ANY)],
            out_specs=pl.BlockSpec(memory_space=pl.ANY),
            scratch_shapes=[pltpu.SemaphoreType.DMA],
        ),
        input_output_aliases={3: 0},
        compiler_params=_params(("arbitrary",), 2 * tm * d * 4),
        name="moe_dispatch",
    )(p1, p2, h, xs0)


def _moe_kernel(te_ref, tv_ref, xs_ref, wg_ref, wu_ref, wd_ref, ys_ref, h_ref, acc_ref):
    i, j, nj = pl.program_id(0), pl.program_id(1), pl.num_programs(1)
    valid = tv_ref[i] > 0

    @pl.when(jnp.logical_and(valid, j == 0))
    def _():
        half = xs_ref.shape[1]
        lo, hi = _unpack_rows(xs_ref[...])
        h_ref[:, :half] = lo.astype(BF16)
        h_ref[:, half:] = hi.astype(BF16)
        acc_ref[...] = jnp.zeros_like(acc_ref)

    @pl.when(valid)
    def _():
        _swiglu_step(h_ref, wg_ref.at[0], wu_ref.at[0], wd_ref.at[0], acc_ref)

    @pl.when(j == nj - 1)
    def _():
        @pl.when(valid)
        def _():
            ys_ref[...] = _pack_rows(acc_ref[...])

        @pl.when(jnp.logical_not(valid))
        def _():
            ys_ref[...] = jnp.zeros_like(ys_ref)


def _moe_experts(xs, tile_e, tile_v, wg, wu, wd, tm):
    r, dh = xs.shape
    d = 2 * dh
    f = wg.shape[2]
    tf = _tile(f, 256)
    nf = f // tf

    def fblk(i, j, te, tv):
        return jnp.where(tv[i] > 0, j, nf - 1)

    est = 4 * tm * dh * 4 + tm * d * 2 + tm * d * 4 + 3 * d * tf * (2 * 4 + 2) + 4 * tm * tf * 4
    return pl.pallas_call(
        _moe_kernel,
        out_shape=jax.ShapeDtypeStruct((r, dh), jnp.uint32),
        grid_spec=pltpu.PrefetchScalarGridSpec(
            num_scalar_prefetch=2,
            grid=(r // tm, nf),
            in_specs=[pl.BlockSpec((tm, dh), lambda i, j, te, tv: (i, 0)),
                      pl.BlockSpec((1, d, tf), lambda i, j, te, tv: (te[i], 0, fblk(i, j, te, tv))),
                      pl.BlockSpec((1, d, tf), lambda i, j, te, tv: (te[i], 0, fblk(i, j, te, tv))),
                      pl.BlockSpec((1, tf, d), lambda i, j, te, tv: (te[i], fblk(i, j, te, tv), 0))],
            out_specs=pl.BlockSpec((tm, dh), lambda i, j, te, tv: (i, 0)),
            scratch_shapes=[pltpu.VMEM((tm, d), BF16), pltpu.VMEM((tm, d), F32)],
        ),
        compiler_params=_params(("arbitrary", "arbitrary"), est),
        name="moe_experts",
    )(tile_e, tile_v, xs, wg, wu, wd)


def _combine_kernel(p1_ref, p2_ref, ys_ref, x_ref, w_ref, gt_ref, g_ref, o_ref, ybuf, sem):
    i, n = pl.program_id(0), pl.num_programs(0)
    tm = x_ref.shape[0]

    def issue(step, slot):
        base = step * tm

        def body(r, c):
            _row_copy(ys_ref, p1_ref[base + r], ybuf.at[slot, 0], r, sem.at[slot]).start()
            _row_copy(ys_ref, p2_ref[base + r], ybuf.at[slot, 1], r, sem.at[slot]).start()
            return c

        lax.fori_loop(0, tm, body, 0, unroll=DMA_LOOP_UNROLL)

    @pl.when(i == 0)
    def _():
        issue(0, 0)

    @pl.when(i + 1 < n)
    def _():
        issue(i + 1, (i + 1) % 2)

    slot = i % 2

    def wait(r, c):
        _row_copy(ys_ref, 0, ybuf.at[slot, 0], 0, sem.at[slot]).wait()
        _row_copy(ys_ref, 0, ybuf.at[slot, 1], 0, sem.at[slot]).wait()
        return c

    lax.fori_loop(0, tm, wait, 0, unroll=DMA_LOOP_UNROLL)
    w = w_ref[...]
    half = ybuf.shape[3]
    a_lo, a_hi = _unpack_rows(ybuf[slot, 0])
    b_lo, b_hi = _unpack_rows(ybuf[slot, 1])
    y_lo = w[:, 0:1] * a_lo + w[:, 1:2] * b_lo
    y_hi = w[:, 0:1] * a_hi + w[:, 1:2] * b_hi
    ms = (jnp.sum(y_lo * y_lo, axis=-1, keepdims=True) + jnp.sum(y_hi * y_hi, axis=-1, keepdims=True)) / (2 * half)
    inv = lax.rsqrt(ms + EPS)
    o_ref[:, :half] = x_ref[:, :half] + gt_ref[0, :, :half] * (y_lo * inv * g_ref[:, :half])
    o_ref[:, half:] = x_ref[:, half:] + gt_ref[0, :, half:] * (y_hi * inv * g_ref[:, half:])


def _combine(ys, p1, p2, x2d, topw, gate, g_post, tokens_per_batch):
    n, d = x2d.shape
    tm = _tile(tokens_per_batch, 512)
    nt = tokens_per_batch // tm
    est = 4 * tm * d * 4 + 4 * tm * (d // 2) * 4 + 4 * tm * d * 4
    return pl.pallas_call(
        _combine_kernel,
        out_shape=jax.ShapeDtypeStruct((n, d), F32),
        grid_spec=pltpu.PrefetchScalarGridSpec(
            num_scalar_prefetch=2,
            grid=(n // tm,),
            in_specs=[pl.BlockSpec(memory_space=pl.ANY),
                      pl.BlockSpec((tm, d), lambda i, p1, p2: (i, 0)),
                      pl.BlockSpec((tm, V7X_LANES), lambda i, p1, p2: (i, 0)),
                      pl.BlockSpec((1, 1, d), lambda i, p1, p2: (i // nt, 0, 0)),
                      pl.BlockSpec((1, d), lambda i, p1, p2: (0, 0))],
            out_specs=pl.BlockSpec((tm, d), lambda i, p1, p2: (i, 0)),
            scratch_shapes=[pltpu.VMEM((2, 2, tm, d // 2), jnp.uint32), pltpu.SemaphoreType.DMA((2,))],
        ),
        compiler_params=_params(("arbitrary",), est),
        name="moe_combine",
    )(p1, p2, ys, x2d, topw, gate, g_post.reshape(1, d))


def _route_plan(e1, e2, tm, n_tiles):
    ar = jnp.arange(N_EXPERTS, dtype=jnp.int32)
    cnt = (e1[:, None] == ar).astype(jnp.int32) + (e2[:, None] == ar).astype(jnp.int32)
    csum = jnp.cumsum(cnt, axis=0)
    padded = ((csum[-1] + tm - 1) // tm) * tm
    ends = jnp.cumsum(padded)
    pos = (ends - padded)[None, :] + csum - 1
    p1 = jnp.take_along_axis(pos, e1[:, None], axis=1)[:, 0]
    p2 = jnp.take_along_axis(pos, e2[:, None], axis=1)[:, 0]
    starts = jnp.arange(n_tiles, dtype=jnp.int32) * tm
    tile_e = jnp.sum((starts[:, None] >= ends[None, :]).astype(jnp.int32), axis=1)
    tile_v = (starts < ends[-1]).astype(jnp.int32)
    last_e = jnp.max(jnp.where(tile_v > 0, tile_e, 0))
    tile_e = jnp.where(tile_v > 0, tile_e, last_e)
    return p1.astype(jnp.int32), p2.astype(jnp.int32), tile_e.astype(jnp.int32), tile_v


def _ffn_moe(x, g_pre, shift, scale, gate, g_post, w_router, wg, wu, wd):
    b, t, d = x.shape
    n = b * t
    wr = jnp.zeros((d, V7X_LANES), F32).at[:, :N_EXPERTS].set(w_router)
    h, topw, topi = _router(x, g_pre, shift, scale, wr)
    tm = _tile(n, 1024)
    n_tiles = (2 * n) // tm + N_EXPERTS
    p1, p2, tile_e, tile_v = _route_plan(topi[:, 0], topi[:, 1], tm, n_tiles)
    xs = _dispatch(h, p1, p2, n_tiles * tm)
    ys = _moe_experts(xs, tile_e, tile_v, wg, wu, wd, tm)
    out = _combine(ys, p1, p2, x.reshape(n, d), topw, gate, g_post, t)
    return out.reshape(b, t, d)


def _deinterleave_heads(w):
    lead, n = w.shape[:-1], w.shape[-1] // HEAD_DIM
    w = w.reshape(*lead, n, HEAD_DIM // 2, 2)
    return jnp.swapaxes(w, -1, -2).reshape(*lead, n * HEAD_DIM)


def _permute_w_in(w_in):
    d = w_in.shape[1]
    kv = d // GROUP
    q, k = _deinterleave_heads(w_in[..., :d]), _deinterleave_heads(w_in[..., d:d + kv])
    return jnp.concatenate([q, w_in[..., d + 2 * kv:], k, w_in[..., d + kv:d + 2 * kv]], axis=-1).astype(BF16)


def _rope_tables(t):
    rows = t // GRID_W
    row = jnp.repeat(jnp.arange(rows), GRID_W).astype(F32)
    col = jnp.tile(jnp.arange(GRID_W), rows).astype(F32)
    n_pairs_axis = HEAD_DIM // 4
    inv = ROPE_THETA ** (-jnp.arange(n_pairs_axis, dtype=F32) / n_pairs_axis)
    ang = jnp.concatenate([row[:, None] * inv[None, :], col[:, None] * inv[None, :]], axis=-1)
    cos, sin = jnp.cos(ang), jnp.sin(ang)
    return jnp.concatenate([cos, cos], axis=-1), jnp.concatenate([-sin, sin], axis=-1)


def _mixer(tok, mod, l, last_unused, w_in_l, gq, gk, w_conv_l, w_pa_l, w_pb_l, w_o_l, g_pre, g_post,
           cc, ss, rope, extra_k=None, extra_v=None):
    del l, last_unused
    b, t, d = tok.shape
    n_heads, n_kv = d // HEAD_DIM, d // HEAD_DIM // GROUP
    kv = n_kv * HEAD_DIM
    sh1, sc1, gt1 = mod
    px = _inproj(tok, g_pre, sh1, sc1, w_in_l)
    q_r, k_r = _qk_norm_rope(px, 0, px, 6 * d // kv, n_heads, n_kv, gq, gk, cc, ss, rope)
    k_all, v_all = k_r, px[:, :, 6 * d + kv:]
    if extra_k is not None:
        k_all = jnp.concatenate([extra_k, k_r], axis=2)
        v_all = jnp.concatenate([extra_v, v_all], axis=1)
    attn = _attention(q_r, jnp.swapaxes(k_all, 2, 3), _v_ext(v_all, n_kv))
    t1 = _aproj(attn, px, 4, w_pa_l)
    m = _bproj(px, (1, 2, 3, 5), t1, w_conv_l, w_pb_l)
    out = _oproj(m, w_o_l, tok, gt1, g_post)
    return out, k_r, px


def kernel(x, c, ctx, c_ctx, w_ada, b_ada, g_pre_mix, g_post_mix, g_pre_ffn, g_post_ffn, w_in, g_q, g_k,
           w_conv, w_pa, w_pb, w_o, w_dense_gate, w_dense_up, w_dense_down, w_router, w_moe_gate, w_moe_up,
           w_moe_down):
    b, t, d = x.shape
    c_len = ctx.shape[1]
    depth = w_in.shape[0]
    n_kv = d // HEAD_DIM // GROUP
    kv = n_kv * HEAD_DIM

    w_in_p = _permute_w_in(w_in)
    gq_p, gk_p = _deinterleave_heads(g_q), _deinterleave_heads(g_k)
    cc, ss = _rope_tables(t)
    cc_c, ss_c = cc[:c_len], ss[:c_len]

    n_rows = -(-(b + 1) // 16) * 16
    c_rows = jnp.zeros((n_rows, d), F32).at[:b].set(c).at[b].set(c_ctx)
    mods = _ada(c_rows, w_ada, b_ada)

    for l in range(depth):
        last = l == depth - 1
        mx = [mods[l, :b, i * d:(i + 1) * d].reshape(b, 1, d) for i in range(6)]
        mc = [mods[l, b:b + 1, i * d:(i + 1) * d].reshape(1, 1, d) for i in range(6)]
        wpa, wpb, wo = w_pa[l].astype(BF16), w_pb[l].astype(BF16), w_o[l].astype(BF16)

        if last:
            pc_kv = _inproj(ctx, g_pre_mix[l], mc[0], mc[1], w_in_p[l][:, 6 * d:])
            _, kc = _qk_norm_rope(pc_kv, 0, pc_kv, 0, n_kv, n_kv, gk_p[l], gk_p[l], cc_c, ss_c, False)
            vc = pc_kv[:, :, kv:]
            ctx_new = None
        else:
            ctx_new, kc, pc = _mixer(ctx, (mc[0], mc[1], mc[2]), l, None, w_in_p[l], gq_p[l], gk_p[l], w_conv[l],
                                     wpa, wpb, wo, g_pre_mix[l], g_post_mix[l], cc_c, ss_c, False)
            vc = pc[:, :, 6 * d + kv:]
        x, _, _ = _mixer(x, (mx[0], mx[1], mx[2]), l, None, w_in_p[l], gq_p[l], gk_p[l], w_conv[l],
                         wpa, wpb, wo, g_pre_mix[l], g_post_mix[l], cc, ss, True, extra_k=kc, extra_v=vc)
        if not last:
            ctx = ctx_new

        i = l // 2
        if l % 2 == 0:
            wg, wu, wd = (w_dense_gate[i].astype(BF16), w_dense_up[i].astype(BF16), w_dense_down[i].astype(BF16))
            x = _ffn_dense(x, g_pre_ffn[l], mx[3], mx[4], mx[5], g_post_ffn[l], wg, wu, wd)
            if not last:
                ctx = _ffn_dense(ctx, g_pre_ffn[l], mc[3], mc[4], mc[5], g_post_ffn[l], wg, wu, wd)
        else:
            wg, wu, wd = w_moe_gate[i], w_moe_up[i], w_moe_down[i]
            x = _ffn_moe(x, g_pre_ffn[l], mx[3], mx[4], mx[5], g_post_ffn[l], w_router[i], wg, wu, wd)
            if not last:
                ctx = _ffn_moe(ctx, g_pre_ffn[l], jnp.broadcast_to(mc[3], (b, 1, d)), jnp.broadcast_to(mc[4], (b, 1, d)),
                               jnp.broadcast_to(mc[5], (b, 1, d)), g_post_ffn[l], w_router[i], wg, wu, wd)
    return x
```

```python
import functools

import jax
import jax.numpy as jnp
from jax import lax
from jax.experimental import pallas as pl
from jax.experimental.pallas import tpu as pltpu

HEAD_DIM = 128
GROUP = 4
GRID_W = 64
N_EXPERTS = 8
EPS = 1e-6
ROPE_THETA = 10000.0
LOG2_E = 1.4426950408889634

V7X_LANES = 128
V7X_VMEM_BYTES = 64 * 1024 * 1024
VMEM_RESERVE_BYTES = 8 * 1024 * 1024

F32 = jnp.float32
BF16 = jnp.bfloat16


def _params(semantics, est_bytes):
    limit = int(min(V7X_VMEM_BYTES - VMEM_RESERVE_BYTES, max(32 * 1024 * 1024, 2 * est_bytes)))
    return pltpu.CompilerParams(dimension_semantics=semantics, vmem_limit_bytes=limit)


def _tile(n, pref):
    t = min(n, pref)
    while n % t:
        t //= 2
    return t


def _rms(x32, g):
    ms = jnp.mean(x32 * x32, axis=-1, keepdims=True)
    return x32 * lax.rsqrt(ms + EPS) * g


def _sigmoid(x):
    return 1.0 / (1.0 + jnp.exp(-x))


NORM_CHUNK_ROWS = 256


def _prenorm_to(h_ref, x_ref, g_ref, sh_ref, sc_ref):
    tm = h_ref.shape[0]
    rows = min(tm, NORM_CHUNK_ROWS)

    def body(r, c):
        sl = pl.ds(pl.multiple_of(r * rows, rows), rows)
        y = _rms(x_ref[0, sl, :], g_ref[...])
        h_ref[sl, :] = (y * (1.0 + sc_ref[0]) + sh_ref[0]).astype(h_ref.dtype)
        return c

    lax.fori_loop(0, tm // rows, body, 0)


def _ada_kernel(c_ref, w_ref, b_ref, o_ref):
    c = c_ref[...]
    s = (c * _sigmoid(c)).astype(BF16)
    w = w_ref[0].astype(BF16)
    o_ref[0] = jnp.dot(s, w, preferred_element_type=F32) + b_ref[0]


def _ada(c_rows, w_ada, b_ada):
    depth, d, n = w_ada.shape
    r = c_rows.shape[0]
    tn = _tile(n, 1024)
    est = 2 * d * tn * 4 + 2 * r * tn * 4 + r * d * 4
    return pl.pallas_call(
        _ada_kernel,
        out_shape=jax.ShapeDtypeStruct((depth, r, n), F32),
        grid=(depth, n // tn),
        in_specs=[
            pl.BlockSpec((r, d), lambda l, j: (0, 0)),
            pl.BlockSpec((1, d, tn), lambda l, j: (l, 0, j)),
            pl.BlockSpec((1, 1, tn), lambda l, j: (l, 0, j)),
        ],
        out_specs=pl.BlockSpec((1, r, tn), lambda l, j: (l, 0, j)),
        compiler_params=_params(("parallel", "parallel"), est),
        name="ada_mod",
    )(c_rows, w_ada, b_ada.reshape(depth, 1, n))


def _inproj_kernel(x_ref, g_ref, sh_ref, sc_ref, w_ref, o_ref, h_ref):
    @pl.when(pl.program_id(2) == 0)
    def _():
        _prenorm_to(h_ref, x_ref, g_ref, sh_ref, sc_ref)

    o_ref[0] = jnp.dot(h_ref[...], w_ref[...], preferred_element_type=F32).astype(o_ref.dtype)


def _inproj(x, g, shift, scale, w):
    b, t, d = x.shape
    n = w.shape[1]
    tm = _tile(t, 1024)
    tn = _tile(n, 1024)
    per_b = shift.shape[0] > 1
    mod_map = (lambda bi, i, j: (bi, 0, 0)) if per_b else (lambda bi, i, j: (0, 0, 0))
    est = 2 * tm * d * 4 + tm * d * 2 + 2 * d * tn * 2 + 2 * tm * tn * 2 + 3 * tm * d * 4
    return pl.pallas_call(
        _inproj_kernel,
        out_shape=jax.ShapeDtypeStruct((b, t, n), BF16),
        grid=(b, t // tm, n // tn),
        in_specs=[
            pl.BlockSpec((1, tm, d), lambda bi, i, j: (bi, i, 0)),
            pl.BlockSpec((1, d), lambda bi, i, j: (0, 0)),
            pl.BlockSpec((1, 1, d), mod_map),
            pl.BlockSpec((1, 1, d), mod_map),
            pl.BlockSpec((d, tn), lambda bi, i, j: (0, j)),
        ],
        out_specs=pl.BlockSpec((1, tm, tn), lambda bi, i, j: (bi, i, j)),
        scratch_shapes=[pltpu.VMEM((tm, d), BF16)],
        compiler_params=_params(("parallel", "parallel", "arbitrary"), est),
        name="inproj",
    )(x, g.reshape(1, d), shift, scale, w)


def _qk_kernel(q_ref, k_ref, gq_ref, gk_ref, cc_ref, ss_ref, qo_ref, ko_ref, *, rope, q_scale):
    ones = jnp.ones((HEAD_DIM, HEAD_DIM), BF16)

    def head(t, g, scale):
        t32 = t.astype(F32)
        ss_all = jnp.dot((t32 * t32).astype(BF16), ones, preferred_element_type=F32)
        y = t32 * lax.rsqrt(ss_all * (1.0 / HEAD_DIM) + EPS) * g
        if rope:
            y = y * cc_ref[...] + pltpu.roll(y, HEAD_DIM // 2, 1) * ss_ref[...]
        if scale != 1.0:
            y = y * scale
        return y.astype(BF16)

    for hh in range(qo_ref.shape[1]):
        qo_ref[0, hh] = head(q_ref[0, :, hh * HEAD_DIM:(hh + 1) * HEAD_DIM], gq_ref[...], q_scale)
    for hh in range(ko_ref.shape[1]):
        ko_ref[0, hh] = head(k_ref[0, :, hh * HEAD_DIM:(hh + 1) * HEAD_DIM], gk_ref[...], 1.0)


def _qk_norm_rope(pq, q_blk, pk, k_blk, n_heads, n_kv, gq, gk, cc, ss, rope):
    b, t = pq.shape[:2]
    tm = _tile(t, 512)
    qw, kw = n_heads * HEAD_DIM, n_kv * HEAD_DIM
    est = 2 * tm * (qw + kw) * 2 * 2 + 6 * tm * HEAD_DIM * 4
    kern = functools.partial(_qk_kernel, rope=rope, q_scale=LOG2_E * HEAD_DIM ** -0.5)
    return pl.pallas_call(
        kern,
        out_shape=(jax.ShapeDtypeStruct((b, n_heads, t, HEAD_DIM), BF16),
                   jax.ShapeDtypeStruct((b, n_kv, t, HEAD_DIM), BF16)),
        grid=(b, t // tm),
        in_specs=[
            pl.BlockSpec((1, tm, qw), lambda bi, i: (bi, i, q_blk)),
            pl.BlockSpec((1, tm, kw), lambda bi, i: (bi, i, k_blk)),
            pl.BlockSpec((1, HEAD_DIM), lambda bi, i: (0, 0)),
            pl.BlockSpec((1, HEAD_DIM), lambda bi, i: (0, 0)),
            pl.BlockSpec((tm, HEAD_DIM), lambda bi, i: (i, 0)),
            pl.BlockSpec((tm, HEAD_DIM), lambda bi, i: (i, 0)),
        ],
        out_specs=(pl.BlockSpec((1, n_heads, tm, HEAD_DIM), lambda bi, i: (bi, 0, i, 0)),
                   pl.BlockSpec((1, n_kv, tm, HEAD_DIM), lambda bi, i: (bi, 0, i, 0))),
        compiler_params=_params(("parallel", "parallel"), est),
        name="qk_norm_rope" if rope else "qk_norm",
    )(pq, pk, gq.reshape(1, HEAD_DIM), gk.reshape(1, HEAD_DIM), cc, ss)


def _attn_kernel(q_ref, kt_ref, v_ref, o_ref):
    kt, v = kt_ref[0, 0], v_ref[0, 0]
    for gi in range(q_ref.shape[1]):
        s = jnp.dot(q_ref[0, gi], kt, preferred_element_type=F32)
        m = jnp.max(s, axis=-1, keepdims=True)
        p = jnp.exp2(s - m).astype(BF16)
        o = jnp.dot(p, v, preferred_element_type=F32)
        o_ref[0, :, gi * HEAD_DIM:(gi + 1) * HEAD_DIM] = (
            o[:, :HEAD_DIM] / o[:, HEAD_DIM:HEAD_DIM + 1]).astype(o_ref.dtype)


def _attention(q, kt, v_ext):
    b, h, t, _ = q.shape
    n_kv, l = kt.shape[1], kt.shape[3]
    tq = _tile(t, 1024)
    gw = GROUP * HEAD_DIM
    est = 2 * (GROUP * tq * HEAD_DIM * 2 + 3 * HEAD_DIM * l * 2 + tq * gw * 2) + GROUP * tq * l * 8
    return pl.pallas_call(
        _attn_kernel,
        out_shape=jax.ShapeDtypeStruct((b, t, h * HEAD_DIM), BF16),
        grid=(b, n_kv, t // tq),
        in_specs=[
            pl.BlockSpec((1, GROUP, tq, HEAD_DIM), lambda bi, kv, i: (bi, kv, i, 0)),
            pl.BlockSpec((1, 1, HEAD_DIM, l), lambda bi, kv, i: (bi, kv, 0, 0)),
            pl.BlockSpec((1, 1, l, 2 * HEAD_DIM), lambda bi, kv, i: (bi, kv, 0, 0)),
        ],
        out_specs=pl.BlockSpec((1, tq, gw), lambda bi, kv, i: (bi, i, kv)),
        compiler_params=_params(("parallel", "parallel", "parallel"), est),
        name="gqa_attention",
    )(q, kt, v_ext)


def _v_ext(v, n_kv):
    b, l, _ = v.shape
    vh = jnp.swapaxes(v.reshape(b, l, n_kv, HEAD_DIM), 1, 2)
    return jnp.concatenate([vh, jnp.ones_like(vh)], axis=-1)


def _aproj_kernel(a_ref, ga_ref, w_ref, o_ref):
    t = jnp.dot(a_ref[0], w_ref[...], preferred_element_type=F32)
    o_ref[0] = (_sigmoid(ga_ref[0].astype(F32)) * t).astype(o_ref.dtype)


def _aproj(attn, px, ga_blk, w_pa):
    b, t, d = attn.shape
    tm = _tile(t, 512)
    est = 2 * d * d * 2 + 2 * 3 * tm * d * 2 + 3 * tm * d * 4
    return pl.pallas_call(
        _aproj_kernel,
        out_shape=jax.ShapeDtypeStruct((b, t, d), BF16),
        grid=(b, t // tm),
        in_specs=[
            pl.BlockSpec((1, tm, d), lambda bi, i: (bi, i, 0)),
            pl.BlockSpec((1, tm, d), lambda bi, i: (bi, i, ga_blk)),
            pl.BlockSpec((d, d), lambda bi, i: (0, 0)),
        ],
        out_specs=pl.BlockSpec((1, tm, d), lambda bi, i: (bi, i, 0)),
        compiler_params=_params(("parallel", "parallel"), est),
        name="attn_proj",
    )(attn, px, w_pa)


HALO_ROWS = 16


def _bproj_kernel(cin_ref, cout_ref, cx_ref, cin_p_ref, cx_p_ref, cin_n_ref, cx_n_ref,
                  gb_ref, t1_ref, wc_ref, w_ref, o_ref):
    i, n = pl.program_id(1), pl.num_programs(1)
    tm = cin_ref.shape[1]
    z = cin_ref[0].astype(F32) * cx_ref[0].astype(F32)
    z_prev_row = cin_p_ref[0, HALO_ROWS - 1:HALO_ROWS, :].astype(F32) * cx_p_ref[0, HALO_ROWS - 1:HALO_ROWS, :].astype(F32)
    z_next_row = cin_n_ref[0, 0:1, :].astype(F32) * cx_n_ref[0, 0:1, :].astype(F32)
    z_prev_row = jnp.where(i > 0, z_prev_row, 0.0)
    z_next_row = jnp.where(i < n - 1, z_next_row, 0.0)
    row = lax.broadcasted_iota(jnp.int32, z.shape, 0)
    z_m1 = jnp.where(row == 0, z_prev_row, pltpu.roll(z, 1, 0))
    z_p1 = jnp.where(row == tm - 1, z_next_row, pltpu.roll(z, tm - 1, 0))
    conv = wc_ref[0:1, :] * z_m1 + wc_ref[1:2, :] * z + wc_ref[2:3, :] * z_p1
    cb = (cout_ref[0].astype(F32) * conv).astype(BF16)
    t = jnp.dot(cb, w_ref[...], preferred_element_type=F32)
    o_ref[0] = (t1_ref[0].astype(F32) + _sigmoid(gb_ref[0].astype(F32)) * t).astype(o_ref.dtype)


def _bproj(px, blks, t1, w_conv, w_pb):
    b, t, _ = px.shape
    d = t1.shape[2]
    tm = _tile(t, 512)
    hb = tm // HALO_ROWS
    n_h = t // HALO_ROWS
    cin_b, cout_b, cx_b, gb_b = blks

    def main(blk):
        return pl.BlockSpec((1, tm, d), lambda bi, i: (bi, i, blk))

    def prev(blk):
        return pl.BlockSpec((1, HALO_ROWS, d), lambda bi, i: (bi, jnp.maximum(i * hb - 1, 0), blk))

    def nxt(blk):
        return pl.BlockSpec((1, HALO_ROWS, d), lambda bi, i: (bi, jnp.minimum((i + 1) * hb, n_h - 1), blk))

    est = 2 * d * d * 2 + 2 * 6 * tm * d * 2 + 6 * tm * d * 4
    return pl.pallas_call(
        _bproj_kernel,
        out_shape=jax.ShapeDtypeStruct((b, t, d), BF16),
        grid=(b, t // tm),
        in_specs=[main(cin_b), main(cout_b), main(cx_b), prev(cin_b), prev(cx_b), nxt(cin_b), nxt(cx_b),
                  main(gb_b),
                  pl.BlockSpec((1, tm, d), lambda bi, i: (bi, i, 0)),
                  pl.BlockSpec((3, d), lambda bi, i: (0, 0)),
                  pl.BlockSpec((d, d), lambda bi, i: (0, 0))],
        out_specs=pl.BlockSpec((1, tm, d), lambda bi, i: (bi, i, 0)),
        compiler_params=_params(("parallel", "parallel"), est),
        name="conv_proj",
    )(px, px, px, px, px, px, px, px, t1, w_conv, w_pb)


def _oproj_kernel(m_ref, w_ref, x_ref, gt_ref, g_ref, o_ref):
    mix = jnp.dot(m_ref[0], w_ref[...], preferred_element_type=F32)
    o_ref[0] = x_ref[0] + gt_ref[0] * _rms(mix, g_ref[...])


def _oproj(m, w_o, x, gate, g_post):
    b, t, d = x.shape
    tm = _tile(t, 512)
    per_b = gate.shape[0] > 1
    gmap = (lambda bi, i: (bi, 0, 0)) if per_b else (lambda bi, i: (0, 0, 0))
    est = 2 * d * d * 2 + 2 * tm * d * (2 + 4 + 4) + 3 * tm * d * 4
    return pl.pallas_call(
        _oproj_kernel,
        out_shape=jax.ShapeDtypeStruct((b, t, d), F32),
        grid=(b, t // tm),
        in_specs=[
            pl.BlockSpec((1, tm, d), lambda bi, i: (bi, i, 0)),
            pl.BlockSpec((d, d), lambda bi, i: (0, 0)),
            pl.BlockSpec((1, tm, d), lambda bi, i: (bi, i, 0)),
            pl.BlockSpec((1, 1, d), gmap),
            pl.BlockSpec((1, d), lambda bi, i: (0, 0)),
        ],
        out_specs=pl.BlockSpec((1, tm, d), lambda bi, i: (bi, i, 0)),
        compiler_params=_params(("parallel", "parallel"), est),
        name="out_proj",
    )(m, w_o, x, gate, g_post.reshape(1, d))


def _swiglu_step(h_ref, wg_ref, wu_ref, wd_ref, acc_ref):
    h = h_ref[...]
    g = jnp.dot(h, wg_ref[...].astype(BF16), preferred_element_type=F32)
    u = jnp.dot(h, wu_ref[...].astype(BF16), preferred_element_type=F32)
    a = (g * _sigmoid(g) * u).astype(BF16)
    acc_ref[...] += jnp.dot(a, wd_ref[...].astype(BF16), preferred_element_type=F32)


def _ffn_kernel(x_ref, g1_ref, sh_ref, sc_ref, gt_ref, g2_ref, wg_ref, wu_ref, wd_ref, o_ref, h_ref, acc_ref):
    j, nj = pl.program_id(2), pl.num_programs(2)

    @pl.when(j == 0)
    def _():
        _prenorm_to(h_ref, x_ref, g1_ref, sh_ref, sc_ref)
        acc_ref[...] = jnp.zeros_like(acc_ref)

    _swiglu_step(h_ref, wg_ref, wu_ref, wd_ref, acc_ref)

    @pl.when(j == nj - 1)
    def _():
        o_ref[0] = x_ref[0] + gt_ref[0] * _rms(acc_ref[...], g2_ref[...])


def _ffn_dense(x, g_pre, shift, scale, gate, g_post, wg, wu, wd):
    b, t, d = x.shape
    f = wg.shape[1]
    tm = _tile(t, 512)
    tf = _tile(f, 1024)
    per_b = gate.shape[0] > 1
    mmap = (lambda bi, i, j: (bi, 0, 0)) if per_b else (lambda bi, i, j: (0, 0, 0))
    vec = pl.BlockSpec((1, d), lambda bi, i, j: (0, 0))
    mod = pl.BlockSpec((1, 1, d), mmap)
    est = 4 * tm * d * 4 + tm * d * 2 + tm * d * 4 + 2 * 3 * d * tf * 2 + 4 * tm * tf * 4 + 2 * tm * d * 4
    return pl.pallas_call(
        _ffn_kernel,
        out_shape=jax.ShapeDtypeStruct((b, t, d), F32),
        grid=(b, t // tm, f // tf),
        in_specs=[
            pl.BlockSpec((1, tm, d), lambda bi, i, j: (bi, i, 0)),
            vec, mod, mod, mod, vec,
            pl.BlockSpec((d, tf), lambda bi, i, j: (0, j)),
            pl.BlockSpec((d, tf), lambda bi, i, j: (0, j)),
            pl.BlockSpec((tf, d), lambda bi, i, j: (j, 0)),
        ],
        out_specs=pl.BlockSpec((1, tm, d), lambda bi, i, j: (bi, i, 0)),
        scratch_shapes=[pltpu.VMEM((tm, d), BF16), pltpu.VMEM((tm, d), F32)],
        compiler_params=_params(("parallel", "parallel", "arbitrary"), est),
        name="ffn_dense",
    )(x, g_pre.reshape(1, d), shift, scale, gate, g_post.reshape(1, d), wg, wu, wd)


HI16 = 0xFFFF0000


def _pack_rows(v):
    half = v.shape[1] // 2
    lo = pltpu.bitcast(v[:, :half].astype(BF16).astype(F32), jnp.uint32)
    hi = pltpu.bitcast(v[:, half:].astype(BF16).astype(F32), jnp.uint32)
    return (lo >> 16) | (hi & jnp.uint32(HI16))


def _unpack_rows(u):
    return pltpu.bitcast(u << 16, F32), pltpu.bitcast(u & jnp.uint32(HI16), F32)


def _split_bf16(v):
    hi = v.astype(BF16)
    return hi, (v - hi.astype(F32)).astype(BF16)


def _router_kernel(x_ref, g_ref, sh_ref, sc_ref, wr_ref, h_ref, w_ref, i_ref):
    y = _rms(x_ref[0], g_ref[...])
    h = y * (1.0 + sc_ref[0]) + sh_ref[0]
    h_ref[...] = _pack_rows(h)
    h_hi, h_lo = _split_bf16(h)
    w_hi, w_lo = _split_bf16(wr_ref[...])
    logits = jnp.dot(h_hi, w_hi, preferred_element_type=F32) + (
        jnp.dot(h_lo, w_hi, preferred_element_type=F32) + jnp.dot(h_hi, w_lo, preferred_element_type=F32))
    lane = lax.broadcasted_iota(jnp.int32, logits.shape, 1)
    lane_f = lane.astype(F32)
    neg = jnp.float32(-jnp.inf)
    big = jnp.float32(V7X_LANES)
    lg = jnp.where(lane < N_EXPERTS, logits, neg)
    v1 = jnp.max(lg, axis=-1, keepdims=True)
    i1 = jnp.min(jnp.where(lg == v1, lane_f, big), axis=-1, keepdims=True)
    lg2 = jnp.where(lane_f == i1, neg, lg)
    v2 = jnp.max(lg2, axis=-1, keepdims=True)
    i2 = jnp.min(jnp.where(lg2 == v2, lane_f, big), axis=-1, keepdims=True)
    e = jnp.exp(v2 - v1)
    den = 1.0 + e
    w_ref[...] = jnp.where(lane == 0, 1.0 / den, jnp.where(lane == 1, e / den, 0.0))
    i_ref[...] = jnp.where(lane == 0, i1, jnp.where(lane == 1, i2, 0.0)).astype(jnp.int32)


def _router(x, g_pre, shift, scale, w_router_pad):
    b, t, d = x.shape
    tm = _tile(t, 512)
    nt = t // tm
    est = 2 * tm * d * 4 * 2 + 2 * d * V7X_LANES * 4 + 4 * tm * d * 4
    vec = pl.BlockSpec((1, d), lambda bi, i: (0, 0))
    mod = pl.BlockSpec((1, 1, d), lambda bi, i: (bi, 0, 0))
    row = lambda w: pl.BlockSpec((tm, w), lambda bi, i: (bi * nt + i, 0))
    return pl.pallas_call(
        _router_kernel,
        out_shape=(jax.ShapeDtypeStruct((b * t, d // 2), jnp.uint32),
                   jax.ShapeDtypeStruct((b * t, V7X_LANES), F32),
                   jax.ShapeDtypeStruct((b * t, V7X_LANES), jnp.int32)),
        grid=(b, nt),
        in_specs=[pl.BlockSpec((1, tm, d), lambda bi, i: (bi, i, 0)), vec, mod, mod,
                  pl.BlockSpec((d, V7X_LANES), lambda bi, i: (0, 0))],
        out_specs=(row(d // 2), row(V7X_LANES), row(V7X_LANES)),
        compiler_params=_params(("parallel", "parallel"), est),
        name="moe_router",
    )(x, g_pre.reshape(1, d), shift, scale, w_router_pad)


DMA_LOOP_UNROLL = 8


def _row_copy(src, src_row, dst, dst_row, sem):
    return pltpu.make_async_copy(src.at[pl.ds(src_row, 1)], dst.at[pl.ds(dst_row, 1)], sem)


def _dispatch_kernel(p1_ref, p2_ref, h_ref, xs_in_ref, xs_ref, sem):
    del xs_in_ref
    tm = h_ref.shape[0]
    base = pl.program_id(0) * tm

    def start(r, c):
        _row_copy(h_ref, r, xs_ref, p1_ref[base + r], sem).start()
        _row_copy(h_ref, r, xs_ref, p2_ref[base + r], sem).start()
        return c

    def wait(r, c):
        _row_copy(h_ref, 0, xs_ref, 0, sem).wait()
        _row_copy(h_ref, 0, xs_ref, 0, sem).wait()
        return c

    lax.fori_loop(0, tm, start, 0, unroll=DMA_LOOP_UNROLL)
    lax.fori_loop(0, tm, wait, 0, unroll=DMA_LOOP_UNROLL)


def _dispatch(h, p1, p2, n_rows):
    n, d = h.shape
    tm = _tile(n, 512)
    xs0 = jnp.zeros((n_rows, d), h.dtype)
    return pl.pallas_call(
        _dispatch_kernel,
        out_shape=jax.ShapeDtypeStruct((n_rows, d), h.dtype),
        grid_spec=pltpu.PrefetchScalarGridSpec(
            num_scalar_prefetch=2,
            grid=(n // tm,),
            in_specs=[pl.BlockSpec((tm, d), lambda i, p1, p2: (i, 0)),
                      pl.BlockSpec(memory_space=pl.ANY)],
            out_specs=pl.BlockSpec(memory_space=pl.ANY),
            scratch_shapes=[pltpu.SemaphoreType.DMA],
        ),
        input_output_aliases={3: 0},
        compiler_params=_params(("arbitrary",), 2 * tm * d * 4),
        name="moe_dispatch",
    )(p1, p2, h, xs0)


def _moe_kernel(te_ref, tv_ref, xs_ref, wg_ref, wu_ref, wd_ref, ys_ref, h_ref, acc_ref):
    i, j, nj = pl.program_id(0), pl.program_id(1), pl.num_programs(1)
    valid = tv_ref[i] > 0

    @pl.when(jnp.logical_and(valid, j == 0))
    def _():
        half = xs_ref.shape[1]
        lo, hi = _unpack_rows(xs_ref[...])
        h_ref[:, :half] = lo.astype(BF16)
        h_ref[:, half:] = hi.astype(BF16)
        acc_ref[...] = jnp.zeros_like(acc_ref)

    hm = h_ref.shape[0] // 2
    full = tv_ref[i] > hm

    @pl.when(full)
    def _():
        _swiglu_step(h_ref, wg_ref.at[0], wu_ref.at[0], wd_ref.at[0], acc_ref)

    @pl.when(jnp.logical_and(valid, jnp.logical_not(full)))
    def _():
        _swiglu_step(h_ref.at[pl.ds(0, hm)], wg_ref.at[0], wu_ref.at[0], wd_ref.at[0], acc_ref.at[pl.ds(0, hm)])

    @pl.when(j == nj - 1)
    def _():
        @pl.when(valid)
        def _():
            ys_ref[...] = _pack_rows(acc_ref[...])

        @pl.when(jnp.logical_not(valid))
        def _():
            ys_ref[...] = jnp.zeros_like(ys_ref)


def _moe_experts(xs, tile_e, tile_v, wg, wu, wd, tm):
    r, dh = xs.shape
    d = 2 * dh
    f = wg.shape[2]
    tf = _tile(f, 256)
    nf = f // tf

    def fblk(i, j, te, tv):
        return jnp.where(tv[i] > 0, j, nf - 1)

    est = 4 * tm * dh * 4 + tm * d * 2 + tm * d * 4 + 3 * d * tf * (2 * 4 + 2) + 4 * tm * tf * 4
    return pl.pallas_call(
        _moe_kernel,
        out_shape=jax.ShapeDtypeStruct((r, dh), jnp.uint32),
        grid_spec=pltpu.PrefetchScalarGridSpec(
            num_scalar_prefetch=2,
            grid=(r // tm, nf),
            in_specs=[pl.BlockSpec((tm, dh), lambda i, j, te, tv: (i, 0)),
                      pl.BlockSpec((1, d, tf), lambda i, j, te, tv: (te[i], 0, fblk(i, j, te, tv))),
                      pl.BlockSpec((1, d, tf), lambda i, j, te, tv: (te[i], 0, fblk(i, j, te, tv))),
                      pl.BlockSpec((1, tf, d), lambda i, j, te, tv: (te[i], fblk(i, j, te, tv), 0))],
            out_specs=pl.BlockSpec((tm, dh), lambda i, j, te, tv: (i, 0)),
            scratch_shapes=[pltpu.VMEM((tm, d), BF16), pltpu.VMEM((tm, d), F32)],
        ),
        compiler_params=_params(("arbitrary", "arbitrary"), est),
        name="moe_experts",
    )(tile_e, tile_v, xs, wg, wu, wd)


def _combine_kernel(p1_ref, p2_ref, ys_ref, x_ref, w_ref, gt_ref, g_ref, o_ref, ybuf, sem):
    i, n = pl.program_id(0), pl.num_programs(0)
    tm = x_ref.shape[0]

    def issue(step, slot):
        base = step * tm

        def body(r, c):
            _row_copy(ys_ref, p1_ref[base + r], ybuf.at[slot, 0], r, sem.at[slot]).start()
            _row_copy(ys_ref, p2_ref[base + r], ybuf.at[slot, 1], r, sem.at[slot]).start()
            return c

        lax.fori_loop(0, tm, body, 0, unroll=DMA_LOOP_UNROLL)

    @pl.when(i == 0)
    def _():
        issue(0, 0)

    @pl.when(i + 1 < n)
    def _():
        issue(i + 1, (i + 1) % 2)

    slot = i % 2

    def wait(r, c):
        _row_copy(ys_ref, 0, ybuf.at[slot, 0], 0, sem.at[slot]).wait()
        _row_copy(ys_ref, 0, ybuf.at[slot, 1], 0, sem.at[slot]).wait()
        return c

    lax.fori_loop(0, tm, wait, 0, unroll=DMA_LOOP_UNROLL)
    w = w_ref[...]
    half = ybuf.shape[3]
    a_lo, a_hi = _unpack_rows(ybuf[slot, 0])
    b_lo, b_hi = _unpack_rows(ybuf[slot, 1])
    y_lo = w[:, 0:1] * a_lo + w[:, 1:2] * b_lo
    y_hi = w[:, 0:1] * a_hi + w[:, 1:2] * b_hi
    ms = (jnp.sum(y_lo * y_lo, axis=-1, keepdims=True) + jnp.sum(y_hi * y_hi, axis=-1, keepdims=True)) / (2 * half)
    inv = lax.rsqrt(ms + EPS)
    o_ref[:, :half] = x_ref[:, :half] + gt_ref[0, :, :half] * (y_lo * inv * g_ref[:, :half])
    o_ref[:, half:] = x_ref[:, half:] + gt_ref[0, :, half:] * (y_hi * inv * g_ref[:, half:])


def _combine(ys, p1, p2, x2d, topw, gate, g_post, tokens_per_batch):
    n, d = x2d.shape
    tm = _tile(tokens_per_batch, 512)
    nt = tokens_per_batch // tm
    est = 4 * tm * d * 4 + 4 * tm * (d // 2) * 4 + 4 * tm * d * 4
    return pl.pallas_call(
        _combine_kernel,
        out_shape=jax.ShapeDtypeStruct((n, d), F32),
        grid_spec=pltpu.PrefetchScalarGridSpec(
            num_scalar_prefetch=2,
            grid=(n // tm,),
            in_specs=[pl.BlockSpec(memory_space=pl.ANY),
                      pl.BlockSpec((tm, d), lambda i, p1, p2: (i, 0)),
                      pl.BlockSpec((tm, V7X_LANES), lambda i, p1, p2: (i, 0)),
                      pl.BlockSpec((1, 1, d), lambda i, p1, p2: (i // nt, 0, 0)),
                      pl.BlockSpec((1, d), lambda i, p1, p2: (0, 0))],
            out_specs=pl.BlockSpec((tm, d), lambda i, p1, p2: (i, 0)),
            scratch_shapes=[pltpu.VMEM((2, 2, tm, d // 2), jnp.uint32), pltpu.SemaphoreType.DMA((2,))],
        ),
        compiler_params=_params(("arbitrary",), est),
        name="moe_combine",
    )(p1, p2, ys, x2d, topw, gate, g_post.reshape(1, d))


def _route_plan(e1, e2, tm, n_tiles):
    ar = jnp.arange(N_EXPERTS, dtype=jnp.int32)
    cnt = (e1[:, None] == ar).astype(jnp.int32) + (e2[:, None] == ar).astype(jnp.int32)
    csum = jnp.cumsum(cnt, axis=0)
    padded = ((csum[-1] + tm - 1) // tm) * tm
    ends = jnp.cumsum(padded)
    pos = (ends - padded)[None, :] + csum - 1
    p1 = jnp.take_along_axis(pos, e1[:, None], axis=1)[:, 0]
    p2 = jnp.take_along_axis(pos, e2[:, None], axis=1)[:, 0]
    starts = jnp.arange(n_tiles, dtype=jnp.int32) * tm
    tile_e = jnp.sum((starts[:, None] >= ends[None, :]).astype(jnp.int32), axis=1)
    in_use = starts < ends[-1]
    last_e = jnp.max(jnp.where(in_use, tile_e, 0))
    tile_e = jnp.where(in_use, tile_e, last_e)
    group_end = (ends - padded + csum[-1])[tile_e]
    tile_rows = jnp.where(in_use, jnp.clip(group_end - starts, 0, tm), 0)
    return p1.astype(jnp.int32), p2.astype(jnp.int32), tile_e.astype(jnp.int32), tile_rows.astype(jnp.int32)


def _ffn_moe(x, g_pre, shift, scale, gate, g_post, w_router, wg, wu, wd):
    b, t, d = x.shape
    n = b * t
    wr = jnp.zeros((d, V7X_LANES), F32).at[:, :N_EXPERTS].set(w_router)
    h, topw, topi = _router(x, g_pre, shift, scale, wr)
    tm = _tile(n, 1024)
    n_tiles = (2 * n) // tm + N_EXPERTS
    p1, p2, tile_e, tile_v = _route_plan(topi[:, 0], topi[:, 1], tm, n_tiles)
    xs = _dispatch(h, p1, p2, n_tiles * tm)
    ys = _moe_experts(xs, tile_e, tile_v, wg, wu, wd, tm)
    out = _combine(ys, p1, p2, x.reshape(n, d), topw, gate, g_post, t)
    return out.reshape(b, t, d)


def _deinterleave_heads(w):
    lead, n = w.shape[:-1], w.shape[-1] // HEAD_DIM
    w = w.reshape(*lead, n, HEAD_DIM // 2, 2)
    return jnp.swapaxes(w, -1, -2).reshape(*lead, n * HEAD_DIM)


def _permute_w_in(w_in):
    d = w_in.shape[1]
    kv = d // GROUP
    q, k = _deinterleave_heads(w_in[..., :d]), _deinterleave_heads(w_in[..., d:d + kv])
    return jnp.concatenate([q, w_in[..., d + 2 * kv:], k, w_in[..., d + kv:d + 2 * kv]], axis=-1).astype(BF16)


def _rope_tables(t):
    rows = t // GRID_W
    row = jnp.repeat(jnp.arange(rows), GRID_W).astype(F32)
    col = jnp.tile(jnp.arange(GRID_W), rows).astype(F32)
    n_pairs_axis = HEAD_DIM // 4
    inv = ROPE_THETA ** (-jnp.arange(n_pairs_axis, dtype=F32) / n_pairs_axis)
    ang = jnp.concatenate([row[:, None] * inv[None, :], col[:, None] * inv[None, :]], axis=-1)
    cos, sin = jnp.cos(ang), jnp.sin(ang)
    return jnp.concatenate([cos, cos], axis=-1), jnp.concatenate([-sin, sin], axis=-1)


def _mixer(tok, mod, l, last_unused, w_in_l, gq, gk, w_conv_l, w_pa_l, w_pb_l, w_o_l, g_pre, g_post,
           cc, ss, rope, extra_k=None, extra_v=None):
    del l, last_unused
    b, t, d = tok.shape
    n_heads, n_kv = d // HEAD_DIM, d // HEAD_DIM // GROUP
    kv = n_kv * HEAD_DIM
    sh1, sc1, gt1 = mod
    flat, unflat = _flatteners(b, t, shared=sh1.shape[0] == 1)
    px = unflat(_inproj(flat(tok), g_pre, sh1, sc1, w_in_l))
    q_r, k_r = _qk_norm_rope(px, 0, px, 6 * d // kv, n_heads, n_kv, gq, gk, cc, ss, rope)
    k_all, v_all = k_r, px[:, :, 6 * d + kv:]
    if extra_k is not None:
        k_all = jnp.concatenate([extra_k, k_r], axis=2)
        v_all = jnp.concatenate([extra_v, v_all], axis=1)
    attn = _attention(q_r, jnp.swapaxes(k_all, 2, 3), _v_ext(v_all, n_kv))
    t1 = unflat(_aproj(flat(attn), flat(px), 4, w_pa_l))
    m = _bproj(px, (1, 2, 3, 5), t1, w_conv_l, w_pb_l)
    out = unflat(_oproj(flat(m), w_o_l, flat(tok), gt1, g_post))
    return out, k_r, px


def _flatteners(b, t, shared):
    if not shared:
        return (lambda a: a), (lambda a: a)
    return (lambda a: a.reshape(1, b * t, a.shape[-1])), (lambda a: a.reshape(b, t, a.shape[-1]))


def kernel(x, c, ctx, c_ctx, w_ada, b_ada, g_pre_mix, g_post_mix, g_pre_ffn, g_post_ffn, w_in, g_q, g_k,
           w_conv, w_pa, w_pb, w_o, w_dense_gate, w_dense_up, w_dense_down, w_router, w_moe_gate, w_moe_up,
           w_moe_down):
    b, t, d = x.shape
    c_len = ctx.shape[1]
    depth = w_in.shape[0]
    n_kv = d // HEAD_DIM // GROUP
    kv = n_kv * HEAD_DIM

    w_in_p = _permute_w_in(w_in)
    gq_p, gk_p = _deinterleave_heads(g_q), _deinterleave_heads(g_k)
    cc, ss = _rope_tables(t)
    cc_c, ss_c = cc[:c_len], ss[:c_len]

    n_rows = -(-(b + 1) // 16) * 16
    c_rows = jnp.zeros((n_rows, d), F32).at[:b].set(c).at[b].set(c_ctx)
    mods = _ada(c_rows, w_ada, b_ada)

    for l in range(depth):
        last = l == depth - 1
        mx = [mods[l, :b, i * d:(i + 1) * d].reshape(b, 1, d) for i in range(6)]
        mc = [mods[l, b:b + 1, i * d:(i + 1) * d].reshape(1, 1, d) for i in range(6)]
        wpa, wpb, wo = w_pa[l].astype(BF16), w_pb[l].astype(BF16), w_o[l].astype(BF16)

        flat_c, unflat_c = _flatteners(b, c_len, shared=True)
        if last:
            pc_kv = unflat_c(_inproj(flat_c(ctx), g_pre_mix[l], mc[0], mc[1], w_in_p[l][:, 6 * d:]))
            _, kc = _qk_norm_rope(pc_kv, 0, pc_kv, 0, n_kv, n_kv, gk_p[l], gk_p[l], cc_c, ss_c, False)
            vc = pc_kv[:, :, kv:]
            ctx_new = None
        else:
            ctx_new, kc, pc = _mixer(ctx, (mc[0], mc[1], mc[2]), l, None, w_in_p[l], gq_p[l], gk_p[l], w_conv[l],
                                     wpa, wpb, wo, g_pre_mix[l], g_post_mix[l], cc_c, ss_c, False)
            vc = pc[:, :, 6 * d + kv:]
        x, _, _ = _mixer(x, (mx[0], mx[1], mx[2]), l, None, w_in_p[l], gq_p[l], gk_p[l], w_conv[l],
                         wpa, wpb, wo, g_pre_mix[l], g_post_mix[l], cc, ss, True, extra_k=kc, extra_v=vc)
        if not last:
            ctx = ctx_new

        i = l // 2
        if l % 2 == 0:
            wg, wu, wd = (w_dense_gate[i].astype(BF16), w_dense_up[i].astype(BF16), w_dense_down[i].astype(BF16))
            x = _ffn_dense(x, g_pre_ffn[l], mx[3], mx[4], mx[5], g_post_ffn[l], wg, wu, wd)
            if not last:
                ctx = unflat_c(_ffn_dense(flat_c(ctx), g_pre_ffn[l], mc[3], mc[4], mc[5], g_post_ffn[l], wg, wu, wd))
        else:
            wg, wu, wd = w_moe_gate[i], w_moe_up[i], w_moe_down[i]
            x = _ffn_moe(x, g_pre_ffn[l], mx[3], mx[4], mx[5], g_post_ffn[l], w_router[i], wg, wu, wd)
            if not last:
                ctx = _ffn_moe(ctx, g_pre_ffn[l], jnp.broadcast_to(mc[3], (b, 1, d)), jnp.broadcast_to(mc[4], (b, 1, d)),
                               jnp.broadcast_to(mc[5], (b, 1, d)), g_post_ffn[l], w_router[i], wg, wu, wd)
    return x
```

```python
import functools

import jax
import jax.numpy as jnp
from jax import lax
from jax.experimental import pallas as pl
from jax.experimental.pallas import tpu as pltpu

HEAD_DIM = 128
GROUP = 4
GRID_W = 64
N_EXPERTS = 8
EPS = 1e-6
ROPE_THETA = 10000.0
LOG2_E = 1.4426950408889634

V7X_LANES = 128
V7X_VMEM_BYTES = 64 * 1024 * 1024
VMEM_RESERVE_BYTES = 8 * 1024 * 1024

F32 = jnp.float32
BF16 = jnp.bfloat16


def _params(semantics, est_bytes):
    limit = int(min(V7X_VMEM_BYTES - VMEM_RESERVE_BYTES, max(32 * 1024 * 1024, 2 * est_bytes)))
    return pltpu.CompilerParams(dimension_semantics=semantics, vmem_limit_bytes=limit)


def _tile(n, pref):
    t = min(n, pref)
    while n % t:
        t //= 2
    return t


def _rms(x32, g):
    ms = jnp.mean(x32 * x32, axis=-1, keepdims=True)
    return x32 * lax.rsqrt(ms + EPS) * g


def _sigmoid(x):
    return 1.0 / (1.0 + jnp.exp(-x))


NORM_CHUNK_ROWS = 256


def _prenorm_to(h_ref, x_ref, g_ref, sh_ref, sc_ref):
    tm = h_ref.shape[0]
    rows = min(tm, NORM_CHUNK_ROWS)

    def body(r, c):
        sl = pl.ds(pl.multiple_of(r * rows, rows), rows)
        y = _rms(x_ref[0, sl, :], g_ref[...])
        h_ref[sl, :] = (y * (1.0 + sc_ref[0]) + sh_ref[0]).astype(h_ref.dtype)
        return c

    lax.fori_loop(0, tm // rows, body, 0)


def _ada_kernel(c_ref, w_ref, b_ref, o_ref):
    c = c_ref[...]
    s = (c * _sigmoid(c)).astype(BF16)
    w = w_ref[0].astype(BF16)
    o_ref[0] = jnp.dot(s, w, preferred_element_type=F32) + b_ref[0]


def _ada(c_rows, w_ada, b_ada):
    depth, d, n = w_ada.shape
    r = c_rows.shape[0]
    tn = _tile(n, 1024)
    est = 2 * d * tn * 4 + 2 * r * tn * 4 + r * d * 4
    return pl.pallas_call(
        _ada_kernel,
        out_shape=jax.ShapeDtypeStruct((depth, r, n), F32),
        grid=(depth, n // tn),
        in_specs=[
            pl.BlockSpec((r, d), lambda l, j: (0, 0)),
            pl.BlockSpec((1, d, tn), lambda l, j: (l, 0, j)),
            pl.BlockSpec((1, 1, tn), lambda l, j: (l, 0, j)),
        ],
        out_specs=pl.BlockSpec((1, r, tn), lambda l, j: (l, 0, j)),
        compiler_params=_params(("parallel", "parallel"), est),
        name="ada_mod",
    )(c_rows, w_ada, b_ada.reshape(depth, 1, n))


def _inproj_kernel(x_ref, g_ref, sh_ref, sc_ref, w_ref, o_ref, h_ref):
    @pl.when(pl.program_id(2) == 0)
    def _():
        _prenorm_to(h_ref, x_ref, g_ref, sh_ref, sc_ref)

    o_ref[0] = jnp.dot(h_ref[...], w_ref[...], preferred_element_type=F32).astype(o_ref.dtype)


def _inproj(x, g, shift, scale, w):
    b, t, d = x.shape
    n = w.shape[1]
    tm = _tile(t, 1024)
    tn = _tile(n, 1024)
    per_b = shift.shape[0] > 1
    mod_map = (lambda bi, i, j: (bi, 0, 0)) if per_b else (lambda bi, i, j: (0, 0, 0))
    est = 2 * tm * d * 4 + tm * d * 2 + 2 * d * tn * 2 + 2 * tm * tn * 2 + 3 * tm * d * 4
    return pl.pallas_call(
        _inproj_kernel,
        out_shape=jax.ShapeDtypeStruct((b, t, n), BF16),
        grid=(b, t // tm, n // tn),
        in_specs=[
            pl.BlockSpec((1, tm, d), lambda bi, i, j: (bi, i, 0)),
            pl.BlockSpec((1, d), lambda bi, i, j: (0, 0)),
            pl.BlockSpec((1, 1, d), mod_map),
            pl.BlockSpec((1, 1, d), mod_map),
            pl.BlockSpec((d, tn), lambda bi, i, j: (0, j)),
        ],
        out_specs=pl.BlockSpec((1, tm, tn), lambda bi, i, j: (bi, i, j)),
        scratch_shapes=[pltpu.VMEM((tm, d), BF16)],
        compiler_params=_params(("parallel", "parallel", "arbitrary"), est),
        name="inproj",
    )(x, g.reshape(1, d), shift, scale, w)


def _qk_kernel(q_ref, k_ref, gq_ref, gk_ref, cc_ref, ss_ref, qo_ref, ko_ref, *, rope, q_scale):
    ones = jnp.ones((HEAD_DIM, HEAD_DIM), BF16)

    def head(t, g, scale):
        t32 = t.astype(F32)
        ss_all = jnp.dot((t32 * t32).astype(BF16), ones, preferred_element_type=F32)
        y = t32 * lax.rsqrt(ss_all * (1.0 / HEAD_DIM) + EPS) * g
        if rope:
            y = y * cc_ref[...] + pltpu.roll(y, HEAD_DIM // 2, 1) * ss_ref[...]
        if scale != 1.0:
            y = y * scale
        return y.astype(BF16)

    for hh in range(qo_ref.shape[1]):
        qo_ref[0, hh] = head(q_ref[0, :, hh * HEAD_DIM:(hh + 1) * HEAD_DIM], gq_ref[...], q_scale)
    for hh in range(ko_ref.shape[1]):
        ko_ref[0, hh] = head(k_ref[0, :, hh * HEAD_DIM:(hh + 1) * HEAD_DIM], gk_ref[...], 1.0)


def _qk_norm_rope(pq, q_blk, pk, k_blk, n_heads, n_kv, gq, gk, cc, ss, rope):
    b, t = pq.shape[:2]
    tm = _tile(t, 512)
    qw, kw = n_heads * HEAD_DIM, n_kv * HEAD_DIM
    est = 2 * tm * (qw + kw) * 2 * 2 + 6 * tm * HEAD_DIM * 4
    kern = functools.partial(_qk_kernel, rope=rope, q_scale=LOG2_E * HEAD_DIM ** -0.5)
    return pl.pallas_call(
        kern,
        out_shape=(jax.ShapeDtypeStruct((b, n_heads, t, HEAD_DIM), BF16),
                   jax.ShapeDtypeStruct((b, n_kv, t, HEAD_DIM), BF16)),
        grid=(b, t // tm),
        in_specs=[
            pl.BlockSpec((1, tm, qw), lambda bi, i: (bi, i, q_blk)),
            pl.BlockSpec((1, tm, kw), lambda bi, i: (bi, i, k_blk)),
            pl.BlockSpec((1, HEAD_DIM), lambda bi, i: (0, 0)),
            pl.BlockSpec((1, HEAD_DIM), lambda bi, i: (0, 0)),
            pl.BlockSpec((tm, HEAD_DIM), lambda bi, i: (i, 0)),
            pl.BlockSpec((tm, HEAD_DIM), lambda bi, i: (i, 0)),
        ],
        out_specs=(pl.BlockSpec((1, n_heads, tm, HEAD_DIM), lambda bi, i: (bi, 0, i, 0)),
                   pl.BlockSpec((1, n_kv, tm, HEAD_DIM), lambda bi, i: (bi, 0, i, 0))),
        compiler_params=_params(("parallel", "parallel"), est),
        name="qk_norm_rope" if rope else "qk_norm",
    )(pq, pk, gq.reshape(1, HEAD_DIM), gk.reshape(1, HEAD_DIM), cc, ss)


def _attn_kernel(q_ref, kt_ref, v_ref, o_ref):
    kt, v = kt_ref[0, 0], v_ref[0, 0]
    for gi in range(q_ref.shape[1]):
        s = jnp.dot(q_ref[0, gi], kt, preferred_element_type=F32)
        m = jnp.max(s, axis=-1, keepdims=True)
        p = jnp.exp2(s - m).astype(BF16)
        o = jnp.dot(p, v, preferred_element_type=F32)
        o_ref[0, :, gi * HEAD_DIM:(gi + 1) * HEAD_DIM] = (
            o[:, :HEAD_DIM] / o[:, HEAD_DIM:HEAD_DIM + 1]).astype(o_ref.dtype)


def _attention(q, kt, v_ext):
    b, h, t, _ = q.shape
    n_kv, l = kt.shape[1], kt.shape[3]
    tq = _tile(t, 1024)
    gw = GROUP * HEAD_DIM
    est = 2 * (GROUP * tq * HEAD_DIM * 2 + 3 * HEAD_DIM * l * 2 + tq * gw * 2) + GROUP * tq * l * 8
    return pl.pallas_call(
        _attn_kernel,
        out_shape=jax.ShapeDtypeStruct((b, t, h * HEAD_DIM), BF16),
        grid=(b, n_kv, t // tq),
        in_specs=[
            pl.BlockSpec((1, GROUP, tq, HEAD_DIM), lambda bi, kv, i: (bi, kv, i, 0)),
            pl.BlockSpec((1, 1, HEAD_DIM, l), lambda bi, kv, i: (bi, kv, 0, 0)),
            pl.BlockSpec((1, 1, l, 2 * HEAD_DIM), lambda bi, kv, i: (bi, kv, 0, 0)),
        ],
        out_specs=pl.BlockSpec((1, tq, gw), lambda bi, kv, i: (bi, i, kv)),
        compiler_params=_params(("parallel", "parallel", "parallel"), est),
        name="gqa_attention",
    )(q, kt, v_ext)


def _v_ext(v, n_kv):
    b, l, _ = v.shape
    vh = jnp.swapaxes(v.reshape(b, l, n_kv, HEAD_DIM), 1, 2)
    return jnp.concatenate([vh, jnp.ones_like(vh)], axis=-1)


def _aproj_kernel(a_ref, ga_ref, w_ref, o_ref):
    t = jnp.dot(a_ref[0], w_ref[...], preferred_element_type=F32)
    o_ref[0] = (_sigmoid(ga_ref[0].astype(F32)) * t).astype(o_ref.dtype)


def _aproj(attn, px, ga_blk, w_pa):
    b, t, d = attn.shape
    tm = _tile(t, 512)
    est = 2 * d * d * 2 + 2 * 3 * tm * d * 2 + 3 * tm * d * 4
    return pl.pallas_call(
        _aproj_kernel,
        out_shape=jax.ShapeDtypeStruct((b, t, d), BF16),
        grid=(b, t // tm),
        in_specs=[
            pl.BlockSpec((1, tm, d), lambda bi, i: (bi, i, 0)),
            pl.BlockSpec((1, tm, d), lambda bi, i: (bi, i, ga_blk)),
            pl.BlockSpec((d, d), lambda bi, i: (0, 0)),
        ],
        out_specs=pl.BlockSpec((1, tm, d), lambda bi, i: (bi, i, 0)),
        compiler_params=_params(("parallel", "parallel"), est),
        name="attn_proj",
    )(attn, px, w_pa)


HALO_ROWS = 16


def _bproj_kernel(cin_ref, cout_ref, cx_ref, cin_p_ref, cx_p_ref, cin_n_ref, cx_n_ref,
                  gb_ref, t1_ref, wc_ref, w_ref, o_ref):
    i, n = pl.program_id(1), pl.num_programs(1)
    tm = cin_ref.shape[1]
    z = cin_ref[0].astype(F32) * cx_ref[0].astype(F32)
    z_prev_row = cin_p_ref[0, HALO_ROWS - 1:HALO_ROWS, :].astype(F32) * cx_p_ref[0, HALO_ROWS - 1:HALO_ROWS, :].astype(F32)
    z_next_row = cin_n_ref[0, 0:1, :].astype(F32) * cx_n_ref[0, 0:1, :].astype(F32)
    z_prev_row = jnp.where(i > 0, z_prev_row, 0.0)
    z_next_row = jnp.where(i < n - 1, z_next_row, 0.0)
    row = lax.broadcasted_iota(jnp.int32, z.shape, 0)
    z_m1 = jnp.where(row == 0, z_prev_row, pltpu.roll(z, 1, 0))
    z_p1 = jnp.where(row == tm - 1, z_next_row, pltpu.roll(z, tm - 1, 0))
    conv = wc_ref[0:1, :] * z_m1 + wc_ref[1:2, :] * z + wc_ref[2:3, :] * z_p1
    cb = (cout_ref[0].astype(F32) * conv).astype(BF16)
    t = jnp.dot(cb, w_ref[...], preferred_element_type=F32)
    o_ref[0] = (t1_ref[0].astype(F32) + _sigmoid(gb_ref[0].astype(F32)) * t).astype(o_ref.dtype)


def _bproj(px, blks, t1, w_conv, w_pb):
    b, t, _ = px.shape
    d = t1.shape[2]
    tm = _tile(t, 512)
    hb = tm // HALO_ROWS
    n_h = t // HALO_ROWS
    cin_b, cout_b, cx_b, gb_b = blks

    def main(blk):
        return pl.BlockSpec((1, tm, d), lambda bi, i: (bi, i, blk))

    def prev(blk):
        return pl.BlockSpec((1, HALO_ROWS, d), lambda bi, i: (bi, jnp.maximum(i * hb - 1, 0), blk))

    def nxt(blk):
        return pl.BlockSpec((1, HALO_ROWS, d), lambda bi, i: (bi, jnp.minimum((i + 1) * hb, n_h - 1), blk))

    est = 2 * d * d * 2 + 2 * 6 * tm * d * 2 + 6 * tm * d * 4
    return pl.pallas_call(
        _bproj_kernel,
        out_shape=jax.ShapeDtypeStruct((b, t, d), BF16),
        grid=(b, t // tm),
        in_specs=[main(cin_b), main(cout_b), main(cx_b), prev(cin_b), prev(cx_b), nxt(cin_b), nxt(cx_b),
                  main(gb_b),
                  pl.BlockSpec((1, tm, d), lambda bi, i: (bi, i, 0)),
                  pl.BlockSpec((3, d), lambda bi, i: (0, 0)),
                  pl.BlockSpec((d, d), lambda bi, i: (0, 0))],
        out_specs=pl.BlockSpec((1, tm, d), lambda bi, i: (bi, i, 0)),
        compiler_params=_params(("parallel", "parallel"), est),
        name="conv_proj",
    )(px, px, px, px, px, px, px, px, t1, w_conv, w_pb)


def _oproj_kernel(m_ref, w_ref, x_ref, gt_ref, g_ref, o_ref):
    mix = jnp.dot(m_ref[0], w_ref[...], preferred_element_type=F32)
    o_ref[0] = x_ref[0] + gt_ref[0] * _rms(mix, g_ref[...])


def _oproj(m, w_o, x, gate, g_post):
    b, t, d = x.shape
    tm = _tile(t, 512)
    per_b = gate.shape[0] > 1
    gmap = (lambda bi, i: (bi, 0, 0)) if per_b else (lambda bi, i: (0, 0, 0))
    est = 2 * d * d * 2 + 2 * tm * d * (2 + 4 + 4) + 3 * tm * d * 4
    return pl.pallas_call(
        _oproj_kernel,
        out_shape=jax.ShapeDtypeStruct((b, t, d), F32),
        grid=(b, t // tm),
        in_specs=[
            pl.BlockSpec((1, tm, d), lambda bi, i: (bi, i, 0)),
            pl.BlockSpec((d, d), lambda bi, i: (0, 0)),
            pl.BlockSpec((1, tm, d), lambda bi, i: (bi, i, 0)),
            pl.BlockSpec((1, 1, d), gmap),
            pl.BlockSpec((1, d), lambda bi, i: (0, 0)),
        ],
        out_specs=pl.BlockSpec((1, tm, d), lambda bi, i: (bi, i, 0)),
        compiler_params=_params(("parallel", "parallel"), est),
        name="out_proj",
    )(m, w_o, x, gate, g_post.reshape(1, d))


def _swiglu_step(h_ref, wg_ref, wu_ref, wd_ref, acc_ref):
    h = h_ref[...]
    g = jnp.dot(h, wg_ref[...].astype(BF16), preferred_element_type=F32)
    u = jnp.dot(h, wu_ref[...].astype(BF16), preferred_element_type=F32)
    a = (g * _sigmoid(g) * u).astype(BF16)
    acc_ref[...] += jnp.dot(a, wd_ref[...].astype(BF16), preferred_element_type=F32)


def _ffn_kernel(x_ref, g1_ref, sh_ref, sc_ref, gt_ref, g2_ref, wg_ref, wu_ref, wd_ref, o_ref, h_ref, acc_ref):
    j, nj = pl.program_id(2), pl.num_programs(2)

    @pl.when(j == 0)
    def _():
        _prenorm_to(h_ref, x_ref, g1_ref, sh_ref, sc_ref)
        acc_ref[...] = jnp.zeros_like(acc_ref)

    _swiglu_step(h_ref, wg_ref, wu_ref, wd_ref, acc_ref)

    @pl.when(j == nj - 1)
    def _():
        o_ref[0] = x_ref[0] + gt_ref[0] * _rms(acc_ref[...], g2_ref[...])


def _ffn_dense(x, g_pre, shift, scale, gate, g_post, wg, wu, wd):
    b, t, d = x.shape
    f = wg.shape[1]
    tm = _tile(t, 512)
    tf = _tile(f, 1024)
    per_b = gate.shape[0] > 1
    mmap = (lambda bi, i, j: (bi, 0, 0)) if per_b else (lambda bi, i, j: (0, 0, 0))
    vec = pl.BlockSpec((1, d), lambda bi, i, j: (0, 0))
    mod = pl.BlockSpec((1, 1, d), mmap)
    est = 4 * tm * d * 4 + tm * d * 2 + tm * d * 4 + 2 * 3 * d * tf * 2 + 4 * tm * tf * 4 + 2 * tm * d * 4
    return pl.pallas_call(
        _ffn_kernel,
        out_shape=jax.ShapeDtypeStruct((b, t, d), F32),
        grid=(b, t // tm, f // tf),
        in_specs=[
            pl.BlockSpec((1, tm, d), lambda bi, i, j: (bi, i, 0)),
            vec, mod, mod, mod, vec,
            pl.BlockSpec((d, tf), lambda bi, i, j: (0, j)),
            pl.BlockSpec((d, tf), lambda bi, i, j: (0, j)),
            pl.BlockSpec((tf, d), lambda bi, i, j: (j, 0)),
        ],
        out_specs=pl.BlockSpec((1, tm, d), lambda bi, i, j: (bi, i, 0)),
        scratch_shapes=[pltpu.VMEM((tm, d), BF16), pltpu.VMEM((tm, d), F32)],
        compiler_params=_params(("parallel", "parallel", "arbitrary"), est),
        name="ffn_dense",
    )(x, g_pre.reshape(1, d), shift, scale, gate, g_post.reshape(1, d), wg, wu, wd)


HI16 = 0xFFFF0000


def _pack_rows(v):
    half = v.shape[1] // 2
    lo = pltpu.bitcast(v[:, :half].astype(BF16).astype(F32), jnp.uint32)
    hi = pltpu.bitcast(v[:, half:].astype(BF16).astype(F32), jnp.uint32)
    return (lo >> 16) | (hi & jnp.uint32(HI16))


def _unpack_rows(u):
    return pltpu.bitcast(u << 16, F32), pltpu.bitcast(u & jnp.uint32(HI16), F32)


def _split_bf16(v):
    hi = v.astype(BF16)
    return hi, (v - hi.astype(F32)).astype(BF16)


def _router_kernel(x_ref, g_ref, sh_ref, sc_ref, wr_ref, h_ref, w_ref, i_ref):
    y = _rms(x_ref[0], g_ref[...])
    h = y * (1.0 + sc_ref[0]) + sh_ref[0]
    h_ref[...] = _pack_rows(h)
    h_hi, h_lo = _split_bf16(h)
    w_hi, w_lo = _split_bf16(wr_ref[...])
    logits = jnp.dot(h_hi, w_hi, preferred_element_type=F32) + (
        jnp.dot(h_lo, w_hi, preferred_element_type=F32) + jnp.dot(h_hi, w_lo, preferred_element_type=F32))
    lane = lax.broadcasted_iota(jnp.int32, logits.shape, 1)
    lane_f = lane.astype(F32)
    neg = jnp.float32(-jnp.inf)
    big = jnp.float32(V7X_LANES)
    lg = jnp.where(lane < N_EXPERTS, logits, neg)
    v1 = jnp.max(lg, axis=-1, keepdims=True)
    i1 = jnp.min(jnp.where(lg == v1, lane_f, big), axis=-1, keepdims=True)
    lg2 = jnp.where(lane_f == i1, neg, lg)
    v2 = jnp.max(lg2, axis=-1, keepdims=True)
    i2 = jnp.min(jnp.where(lg2 == v2, lane_f, big), axis=-1, keepdims=True)
    e = jnp.exp(v2 - v1)
    den = 1.0 + e
    w_ref[...] = jnp.where(lane == 0, 1.0 / den, jnp.where(lane == 1, e / den, 0.0))
    i_ref[...] = jnp.where(lane == 0, i1, jnp.where(lane == 1, i2, 0.0)).astype(jnp.int32)


def _router(x, g_pre, shift, scale, w_router_pad):
    b, t, d = x.shape
    tm = _tile(t, 512)
    nt = t // tm
    est = 2 * tm * d * 4 * 2 + 2 * d * V7X_LANES * 4 + 4 * tm * d * 4
    vec = pl.BlockSpec((1, d), lambda bi, i: (0, 0))
    mod = pl.BlockSpec((1, 1, d), lambda bi, i: (bi, 0, 0))
    row = lambda w: pl.BlockSpec((tm, w), lambda bi, i: (bi * nt + i, 0))
    return pl.pallas_call(
        _router_kernel,
        out_shape=(jax.ShapeDtypeStruct((b * t, d // 2), jnp.uint32),
                   jax.ShapeDtypeStruct((b * t, V7X_LANES), F32),
                   jax.ShapeDtypeStruct((b * t, V7X_LANES), jnp.int32)),
        grid=(b, nt),
        in_specs=[pl.BlockSpec((1, tm, d), lambda bi, i: (bi, i, 0)), vec, mod, mod,
                  pl.BlockSpec((d, V7X_LANES), lambda bi, i: (0, 0))],
        out_specs=(row(d // 2), row(V7X_LANES), row(V7X_LANES)),
        compiler_params=_params(("parallel", "parallel"), est),
        name="moe_router",
    )(x, g_pre.reshape(1, d), shift, scale, w_router_pad)


DMA_LOOP_UNROLL = 8


def _row_copy(src, src_row, dst, dst_row, sem):
    return pltpu.make_async_copy(src.at[pl.ds(src_row, 1)], dst.at[pl.ds(dst_row, 1)], sem)


def _dispatch_kernel(p1_ref, p2_ref, h_ref, xs_in_ref, xs_ref, sem):
    del xs_in_ref
    tm = h_ref.shape[0]
    base = pl.program_id(0) * tm

    def start(r, c):
        _row_copy(h_ref, r, xs_ref, p1_ref[base + r], sem).start()
        _row_copy(h_ref, r, xs_ref, p2_ref[base + r], sem).start()
        return c

    def wait(r, c):
        _row_copy(h_ref, 0, xs_ref, 0, sem).wait()
        _row_copy(h_ref, 0, xs_ref, 0, sem).wait()
        return c

    lax.fori_loop(0, tm, start, 0, unroll=DMA_LOOP_UNROLL)
    lax.fori_loop(0, tm, wait, 0, unroll=DMA_LOOP_UNROLL)


def _dispatch(h, p1, p2, n_rows):
    n, d = h.shape
    tm = _tile(n, 512)
    xs0 = jnp.zeros((n_rows, d), h.dtype)
    return pl.pallas_call(
        _dispatch_kernel,
        out_shape=jax.ShapeDtypeStruct((n_rows, d), h.dtype),
        grid_spec=pltpu.PrefetchScalarGridSpec(
            num_scalar_prefetch=2,
            grid=(n // tm,),
            in_specs=[pl.BlockSpec((tm, d), lambda i, p1, p2: (i, 0)),
                      pl.BlockSpec(memory_space=pl.ANY)],
            out_specs=pl.BlockSpec(memory_space=pl.ANY),
            scratch_shapes=[pltpu.SemaphoreType.DMA],
        ),
        input_output_aliases={3: 0},
        compiler_params=_params(("arbitrary",), 2 * tm * d * 4),
        name="moe_dispatch",
    )(p1, p2, h, xs0)


def _moe_kernel(te_ref, tv_ref, xs_ref, wg_hbm, wu_hbm, wd_hbm, ys_ref, h_ref, acc_ref, wg_buf, wu_buf, wd_buf, sem):
    i, n = pl.program_id(0), pl.num_programs(0)
    tf = wg_buf.shape[2]
    nf = wg_hbm.shape[2] // tf
    rows = tv_ref[i]
    hm = h_ref.shape[0] // 2

    def chunk_copies(tile, c, slot):
        e = te_ref[tile]
        cols = pl.ds(pl.multiple_of(c * tf, tf), tf)
        return (pltpu.make_async_copy(wg_hbm.at[e, :, cols], wg_buf.at[slot], sem.at[0, slot]),
                pltpu.make_async_copy(wu_hbm.at[e, :, cols], wu_buf.at[slot], sem.at[1, slot]),
                pltpu.make_async_copy(wd_hbm.at[e, cols, :], wd_buf.at[slot], sem.at[2, slot]))

    def start(tile, c, slot):
        for cp in chunk_copies(tile, c, slot):
            cp.start()

    @pl.when(jnp.logical_and(i == 0, rows > 0))
    def _():
        start(0, 0, 0)

    @pl.when(rows > 0)
    def _():
        half = xs_ref.shape[1]
        lo, hi = _unpack_rows(xs_ref[...])
        h_ref[:, :half] = lo.astype(BF16)
        h_ref[:, half:] = hi.astype(BF16)
        acc_ref[...] = jnp.zeros_like(acc_ref)
        next_used = jnp.logical_and(i + 1 < n, tv_ref[jnp.minimum(i + 1, n - 1)] > 0)

        def chunk(c, slot):
            @pl.when(c + 1 < nf)
            def _():
                start(i, c + 1, 1 - slot)

            @pl.when(jnp.logical_and(c + 1 == nf, next_used))
            def _():
                start(i + 1, 0, 1 - slot)

            for cp in chunk_copies(i, c, slot):
                cp.wait()

            @pl.when(rows > hm)
            def _():
                _swiglu_step(h_ref, wg_buf.at[slot], wu_buf.at[slot], wd_buf.at[slot], acc_ref)

            @pl.when(rows <= hm)
            def _():
                _swiglu_step(h_ref.at[pl.ds(0, hm)], wg_buf.at[slot], wu_buf.at[slot], wd_buf.at[slot],
                             acc_ref.at[pl.ds(0, hm)])

        def pair(k, carry):
            chunk(2 * k, 0)
            chunk(2 * k + 1, 1)
            return carry

        lax.fori_loop(0, nf // 2, pair, 0)
        ys_ref[...] = _pack_rows(acc_ref[...])

    @pl.when(rows == 0)
    def _():
        ys_ref[...] = jnp.zeros_like(ys_ref)


def _moe_experts(xs, tile_e, tile_v, wg, wu, wd, tm):
    r, dh = xs.shape
    d = 2 * dh
    f = wg.shape[2]
    tf = _tile(f, 256)
    assert (f // tf) % 2 == 0
    est = 4 * tm * dh * 4 + tm * d * 2 + tm * d * 4 + 3 * d * tf * (2 * 4 + 2) + 4 * tm * tf * 4
    return pl.pallas_call(
        _moe_kernel,
        out_shape=jax.ShapeDtypeStruct((r, dh), jnp.uint32),
        grid_spec=pltpu.PrefetchScalarGridSpec(
            num_scalar_prefetch=2,
            grid=(r // tm,),
            in_specs=[pl.BlockSpec((tm, dh), lambda i, te, tv: (i, 0)),
                      pl.BlockSpec(memory_space=pl.ANY),
                      pl.BlockSpec(memory_space=pl.ANY),
                      pl.BlockSpec(memory_space=pl.ANY)],
            out_specs=pl.BlockSpec((tm, dh), lambda i, te, tv: (i, 0)),
            scratch_shapes=[pltpu.VMEM((tm, d), BF16), pltpu.VMEM((tm, d), F32),
                            pltpu.VMEM((2, d, tf), F32), pltpu.VMEM((2, d, tf), F32), pltpu.VMEM((2, tf, d), F32),
                            pltpu.SemaphoreType.DMA((3, 2))],
        ),
        compiler_params=_params(("arbitrary",), est),
        name="moe_experts",
    )(tile_e, tile_v, xs, wg, wu, wd)


def _combine_kernel(p1_ref, p2_ref, ys_ref, x_ref, w_ref, gt_ref, g_ref, o_ref, ybuf, sem):
    i, n = pl.program_id(0), pl.num_programs(0)
    tm = x_ref.shape[0]

    def issue(step, slot):
        base = step * tm

        def body(r, c):
            _row_copy(ys_ref, p1_ref[base + r], ybuf.at[slot, 0], r, sem.at[slot]).start()
            _row_copy(ys_ref, p2_ref[base + r], ybuf.at[slot, 1], r, sem.at[slot]).start()
            return c

        lax.fori_loop(0, tm, body, 0, unroll=DMA_LOOP_UNROLL)

    @pl.when(i == 0)
    def _():
        issue(0, 0)

    @pl.when(i + 1 < n)
    def _():
        issue(i + 1, (i + 1) % 2)

    slot = i % 2

    def wait(r, c):
        _row_copy(ys_ref, 0, ybuf.at[slot, 0], 0, sem.at[slot]).wait()
        _row_copy(ys_ref, 0, ybuf.at[slot, 1], 0, sem.at[slot]).wait()
        return c

    lax.fori_loop(0, tm, wait, 0, unroll=DMA_LOOP_UNROLL)
    w = w_ref[...]
    half = ybuf.shape[3]
    a_lo, a_hi = _unpack_rows(ybuf[slot, 0])
    b_lo, b_hi = _unpack_rows(ybuf[slot, 1])
    y_lo = w[:, 0:1] * a_lo + w[:, 1:2] * b_lo
    y_hi = w[:, 0:1] * a_hi + w[:, 1:2] * b_hi
    ms = (jnp.sum(y_lo * y_lo, axis=-1, keepdims=True) + jnp.sum(y_hi * y_hi, axis=-1, keepdims=True)) / (2 * half)
    inv = lax.rsqrt(ms + EPS)
    o_ref[:, :half] = x_ref[:, :half] + gt_ref[0, :, :half] * (y_lo * inv * g_ref[:, :half])
    o_ref[:, half:] = x_ref[:, half:] + gt_ref[0, :, half:] * (y_hi * inv * g_ref[:, half:])


def _combine(ys, p1, p2, x2d, topw, gate, g_post, tokens_per_batch):
    n, d = x2d.shape
    tm = _tile(tokens_per_batch, 512)
    nt = tokens_per_batch // tm
    est = 4 * tm * d * 4 + 4 * tm * (d // 2) * 4 + 4 * tm * d * 4
    return pl.pallas_call(
        _combine_kernel,
        out_shape=jax.ShapeDtypeStruct((n, d), F32),
        grid_spec=pltpu.PrefetchScalarGridSpec(
            num_scalar_prefetch=2,
            grid=(n // tm,),
            in_specs=[pl.BlockSpec(memory_space=pl.ANY),
                      pl.BlockSpec((tm, d), lambda i, p1, p2: (i, 0)),
                      pl.BlockSpec((tm, V7X_LANES), lambda i, p1, p2: (i, 0)),
                      pl.BlockSpec((1, 1, d), lambda i, p1, p2: (i // nt, 0, 0)),
                      pl.BlockSpec((1, d), lambda i, p1, p2: (0, 0))],
            out_specs=pl.BlockSpec((tm, d), lambda i, p1, p2: (i, 0)),
            scratch_shapes=[pltpu.VMEM((2, 2, tm, d // 2), jnp.uint32), pltpu.SemaphoreType.DMA((2,))],
        ),
        compiler_params=_params(("arbitrary",), est),
        name="moe_combine",
    )(p1, p2, ys, x2d, topw, gate, g_post.reshape(1, d))


def _route_plan(e1, e2, tm, n_tiles):
    ar = jnp.arange(N_EXPERTS, dtype=jnp.int32)
    cnt = (e1[:, None] == ar).astype(jnp.int32) + (e2[:, None] == ar).astype(jnp.int32)
    csum = jnp.cumsum(cnt, axis=0)
    padded = ((csum[-1] + tm - 1) // tm) * tm
    ends = jnp.cumsum(padded)
    pos = (ends - padded)[None, :] + csum - 1
    p1 = jnp.take_along_axis(pos, e1[:, None], axis=1)[:, 0]
    p2 = jnp.take_along_axis(pos, e2[:, None], axis=1)[:, 0]
    starts = jnp.arange(n_tiles, dtype=jnp.int32) * tm
    tile_e = jnp.sum((starts[:, None] >= ends[None, :]).astype(jnp.int32), axis=1)
    in_use = starts < ends[-1]
    last_e = jnp.max(jnp.where(in_use, tile_e, 0))
    tile_e = jnp.where(in_use, tile_e, last_e)
    group_end = (ends - padded + csum[-1])[tile_e]
    tile_rows = jnp.where(in_use, jnp.clip(group_end - starts, 0, tm), 0)
    return p1.astype(jnp.int32), p2.astype(jnp.int32), tile_e.astype(jnp.int32), tile_rows.astype(jnp.int32)


def _ffn_moe(x, g_pre, shift, scale, gate, g_post, w_router, wg, wu, wd):
    b, t, d = x.shape
    n = b * t
    wr = jnp.zeros((d, V7X_LANES), F32).at[:, :N_EXPERTS].set(w_router)
    h, topw, topi = _router(x, g_pre, shift, scale, wr)
    tm = _tile(n, 1024)
    n_tiles = (2 * n) // tm + N_EXPERTS
    p1, p2, tile_e, tile_v = _route_plan(topi[:, 0], topi[:, 1], tm, n_tiles)
    xs = _dispatch(h, p1, p2, n_tiles * tm)
    ys = _moe_experts(xs, tile_e, tile_v, wg, wu, wd, tm)
    out = _combine(ys, p1, p2, x.reshape(n, d), topw, gate, g_post, t)
    return out.reshape(b, t, d)


def _deinterleave_heads(w):
    lead, n = w.shape[:-1], w.shape[-1] // HEAD_DIM
    w = w.reshape(*lead, n, HEAD_DIM // 2, 2)
    return jnp.swapaxes(w, -1, -2).reshape(*lead, n * HEAD_DIM)


def _permute_w_in(w_in):
    d = w_in.shape[1]
    kv = d // GROUP
    q, k = _deinterleave_heads(w_in[..., :d]), _deinterleave_heads(w_in[..., d:d + kv])
    return jnp.concatenate([q, w_in[..., d + 2 * kv:], k, w_in[..., d + kv:d + 2 * kv]], axis=-1).astype(BF16)


def _rope_tables(t):
    rows = t // GRID_W
    row = jnp.repeat(jnp.arange(rows), GRID_W).astype(F32)
    col = jnp.tile(jnp.arange(GRID_W), rows).astype(F32)
    n_pairs_axis = HEAD_DIM // 4
    inv = ROPE_THETA ** (-jnp.arange(n_pairs_axis, dtype=F32) / n_pairs_axis)
    ang = jnp.concatenate([row[:, None] * inv[None, :], col[:, None] * inv[None, :]], axis=-1)
    cos, sin = jnp.cos(ang), jnp.sin(ang)
    return jnp.concatenate([cos, cos], axis=-1), jnp.concatenate([-sin, sin], axis=-1)


def _mixer(tok, mod, l, last_unused, w_in_l, gq, gk, w_conv_l, w_pa_l, w_pb_l, w_o_l, g_pre, g_post,
           cc, ss, rope, extra_k=None, extra_v=None):
    del l, last_unused
    b, t, d = tok.shape
    n_heads, n_kv = d // HEAD_DIM, d // HEAD_DIM // GROUP
    kv = n_kv * HEAD_DIM
    sh1, sc1, gt1 = mod
    flat, unflat = _flatteners(b, t, shared=sh1.shape[0] == 1)
    px = unflat(_inproj(flat(tok), g_pre, sh1, sc1, w_in_l))
    q_r, k_r = _qk_norm_rope(px, 0, px, 6 * d // kv, n_heads, n_kv, gq, gk, cc, ss, rope)
    k_all, v_all = k_r, px[:, :, 6 * d + kv:]
    if extra_k is not None:
        k_all = jnp.concatenate([extra_k, k_r], axis=2)
        v_all = jnp.concatenate([extra_v, v_all], axis=1)
    attn = _attention(q_r, jnp.swapaxes(k_all, 2, 3), _v_ext(v_all, n_kv))
    t1 = unflat(_aproj(flat(attn), flat(px), 4, w_pa_l))
    m = _bproj(px, (1, 2, 3, 5), t1, w_conv_l, w_pb_l)
    out = unflat(_oproj(flat(m), w_o_l, flat(tok), gt1, g_post))
    return out, k_r, px


def _flatteners(b, t, shared):
    if not shared:
        return (lambda a: a), (lambda a: a)
    return (lambda a: a.reshape(1, b * t, a.shape[-1])), (lambda a: a.reshape(b, t, a.shape[-1]))


def kernel(x, c, ctx, c_ctx, w_ada, b_ada, g_pre_mix, g_post_mix, g_pre_ffn, g_post_ffn, w_in, g_q, g_k,
           w_conv, w_pa, w_pb, w_o, w_dense_gate, w_dense_up, w_dense_down, w_router, w_moe_gate, w_moe_up,
           w_moe_down):
    b, t, d = x.shape
    c_len = ctx.shape[1]
    depth = w_in.shape[0]
    n_kv = d // HEAD_DIM // GROUP
    kv = n_kv * HEAD_DIM

    w_in_p = _permute_w_in(w_in)
    gq_p, gk_p = _deinterleave_heads(g_q), _deinterleave_heads(g_k)
    cc, ss = _rope_tables(t)
    cc_c, ss_c = cc[:c_len], ss[:c_len]

    n_rows = -(-(b + 1) // 16) * 16
    c_rows = jnp.zeros((n_rows, d), F32).at[:b].set(c).at[b].set(c_ctx)
    mods = _ada(c_rows, w_ada, b_ada)

    for l in range(depth):
        last = l == depth - 1
        mx = [mods[l, :b, i * d:(i + 1) * d].reshape(b, 1, d) for i in range(6)]
        mc = [mods[l, b:b + 1, i * d:(i + 1) * d].reshape(1, 1, d) for i in range(6)]
        wpa, wpb, wo = w_pa[l].astype(BF16), w_pb[l].astype(BF16), w_o[l].astype(BF16)

        flat_c, unflat_c = _flatteners(b, c_len, shared=True)
        if last:
            pc_kv = unflat_c(_inproj(flat_c(ctx), g_pre_mix[l], mc[0], mc[1], w_in_p[l][:, 6 * d:]))
            _, kc = _qk_norm_rope(pc_kv, 0, pc_kv, 0, n_kv, n_kv, gk_p[l], gk_p[l], cc_c, ss_c, False)
            vc = pc_kv[:, :, kv:]
            ctx_new = None
        else:
            ctx_new, kc, pc = _mixer(ctx, (mc[0], mc[1], mc[2]), l, None, w_in_p[l], gq_p[l], gk_p[l], w_conv[l],
                                     wpa, wpb, wo, g_pre_mix[l], g_post_mix[l], cc_c, ss_c, False)
            vc = pc[:, :, 6 * d + kv:]
        x, _, _ = _mixer(x, (mx[0], mx[1], mx[2]), l, None, w_in_p[l], gq_p[l], gk_p[l], w_conv[l],
                         wpa, wpb, wo, g_pre_mix[l], g_post_mix[l], cc, ss, True, extra_k=kc, extra_v=vc)
        if not last:
            ctx = ctx_new

        i = l // 2
        if l % 2 == 0:
            wg, wu, wd = (w_dense_gate[i].astype(BF16), w_dense_up[i].astype(BF16), w_dense_down[i].astype(BF16))
            x = _ffn_dense(x, g_pre_ffn[l], mx[3], mx[4], mx[5], g_post_ffn[l], wg, wu, wd)
            if not last:
                ctx = unflat_c(_ffn_dense(flat_c(ctx), g_pre_ffn[l], mc[3], mc[4], mc[5], g_post_ffn[l], wg, wu, wd))
        else:
            wg, wu, wd = w_moe_gate[i], w_moe_up[i], w_moe_down[i]
            x = _ffn_moe(x, g_pre_ffn[l], mx[3], mx[4], mx[5], g_post_ffn[l], w_router[i], wg, wu, wd)
            if not last:
                ctx = _ffn_moe(ctx, g_pre_ffn[l], jnp.broadcast_to(mc[3], (b, 1, d)), jnp.broadcast_to(mc[4], (b, 1, d)),
                               jnp.broadcast_to(mc[5], (b, 1, d)), g_post_ffn[l], w_router[i], wg, wu, wd)
    return x
```

```python
import functools

import jax
import jax.numpy as jnp
from jax import lax
from jax.experimental import pallas as pl
from jax.experimental.pallas import tpu as pltpu

HEAD_DIM = 128
GROUP = 4
GRID_W = 64
N_EXPERTS = 8
EPS = 1e-6
ROPE_THETA = 10000.0
LOG2_E = 1.4426950408889634

V7X_LANES = 128
V7X_VMEM_BYTES = 64 * 1024 * 1024
VMEM_RESERVE_BYTES = 8 * 1024 * 1024

F32 = jnp.float32
BF16 = jnp.bfloat16


def _params(semantics, est_bytes):
    limit = int(min(V7X_VMEM_BYTES - VMEM_RESERVE_BYTES, max(32 * 1024 * 1024, 2 * est_bytes)))
    return pltpu.CompilerParams(dimension_semantics=semantics, vmem_limit_bytes=limit)


def _tile(n, pref):
    t = min(n, pref)
    while n % t:
        t //= 2
    return t


def _rms(x32, g):
    ms = jnp.mean(x32 * x32, axis=-1, keepdims=True)
    return x32 * lax.rsqrt(ms + EPS) * g


def _sigmoid(x):
    return 1.0 / (1.0 + jnp.exp(-x))


NORM_CHUNK_ROWS = 256


def _prenorm_to(h_ref, x_ref, g_ref, sh_ref, sc_ref):
    tm = h_ref.shape[0]
    rows = min(tm, NORM_CHUNK_ROWS)

    def body(r, c):
        sl = pl.ds(pl.multiple_of(r * rows, rows), rows)
        y = _rms(x_ref[0, sl, :], g_ref[...])
        h_ref[sl, :] = (y * (1.0 + sc_ref[0]) + sh_ref[0]).astype(h_ref.dtype)
        return c

    lax.fori_loop(0, tm // rows, body, 0)


def _ada_kernel(c_ref, w_ref, b_ref, o_ref):
    c = c_ref[...]
    s = (c * _sigmoid(c)).astype(BF16)
    w = w_ref[0].astype(BF16)
    o_ref[0] = jnp.dot(s, w, preferred_element_type=F32) + b_ref[0]


def _ada(c_rows, w_ada, b_ada):
    depth, d, n = w_ada.shape
    r = c_rows.shape[0]
    tn = _tile(n, 1024)
    est = 2 * d * tn * 4 + 2 * r * tn * 4 + r * d * 4
    return pl.pallas_call(
        _ada_kernel,
        out_shape=jax.ShapeDtypeStruct((depth, r, n), F32),
        grid=(depth, n // tn),
        in_specs=[
            pl.BlockSpec((r, d), lambda l, j: (0, 0)),
            pl.BlockSpec((1, d, tn), lambda l, j: (l, 0, j)),
            pl.BlockSpec((1, 1, tn), lambda l, j: (l, 0, j)),
        ],
        out_specs=pl.BlockSpec((1, r, tn), lambda l, j: (l, 0, j)),
        compiler_params=_params(("parallel", "parallel"), est),
        name="ada_mod",
    )(c_rows, w_ada, b_ada.reshape(depth, 1, n))


def _inproj_kernel(x_ref, g_ref, sh_ref, sc_ref, w_ref, o_ref, h_ref):
    @pl.when(pl.program_id(2) == 0)
    def _():
        _prenorm_to(h_ref, x_ref, g_ref, sh_ref, sc_ref)

    o_ref[0] = jnp.dot(h_ref[...], w_ref[...], preferred_element_type=F32).astype(o_ref.dtype)


def _inproj(x, g, shift, scale, w, kv_after_q=0):
    b, t, d = x.shape
    n = w.shape[1]
    tm = _tile(t, 1024)
    tn = _tile(n, 1024)
    assert d % tn == 0 and kv_after_q % tn == 0
    nq, nk, nb = d // tn, kv_after_q // tn, n // tn

    def out_col(j):
        return jnp.where(j < nq, j, jnp.where(j < nq + nk, j + (nb - nq - nk), j - nk))

    per_b = shift.shape[0] > 1
    mod_map = (lambda bi, i, j: (bi, 0, 0)) if per_b else (lambda bi, i, j: (0, 0, 0))
    est = 2 * tm * d * 4 + tm * d * 2 + 2 * d * tn * 2 + 2 * tm * tn * 2 + 3 * tm * d * 4
    return pl.pallas_call(
        _inproj_kernel,
        out_shape=jax.ShapeDtypeStruct((b, t, n), BF16),
        grid=(b, t // tm, n // tn),
        in_specs=[
            pl.BlockSpec((1, tm, d), lambda bi, i, j: (bi, i, 0)),
            pl.BlockSpec((1, d), lambda bi, i, j: (0, 0)),
            pl.BlockSpec((1, 1, d), mod_map),
            pl.BlockSpec((1, 1, d), mod_map),
            pl.BlockSpec((d, tn), lambda bi, i, j: (0, j)),
        ],
        out_specs=pl.BlockSpec((1, tm, tn), lambda bi, i, j: (bi, i, out_col(j))),
        scratch_shapes=[pltpu.VMEM((tm, d), BF16)],
        compiler_params=_params(("parallel", "parallel", "arbitrary"), est),
        name="inproj",
    )(x, g.reshape(1, d), shift, scale, w)


def _qk_kernel(q_ref, k_ref, gq_ref, gk_ref, cc_ref, ss_ref, qo_ref, ko_ref, *, rope, q_scale):
    ones = jnp.ones((HEAD_DIM, HEAD_DIM), BF16)

    def head(t, g, scale):
        t32 = t.astype(F32)
        ss_all = jnp.dot((t32 * t32).astype(BF16), ones, preferred_element_type=F32)
        y = t32 * lax.rsqrt(ss_all * (1.0 / HEAD_DIM) + EPS) * g
        if rope:
            y = y * cc_ref[...] + pltpu.roll(y, HEAD_DIM // 2, 1) * ss_ref[...]
        if scale != 1.0:
            y = y * scale
        return y.astype(BF16)

    for hh in range(qo_ref.shape[1]):
        qo_ref[0, hh] = head(q_ref[0, :, hh * HEAD_DIM:(hh + 1) * HEAD_DIM], gq_ref[...], q_scale)
    for hh in range(ko_ref.shape[1]):
        ko_ref[0, hh] = head(k_ref[0, :, hh * HEAD_DIM:(hh + 1) * HEAD_DIM], gk_ref[...], 1.0)


def _qk_norm_rope(pq, q_blk, pk, k_blk, n_heads, n_kv, gq, gk, cc, ss, rope):
    b, t = pq.shape[:2]
    tm = _tile(t, 512)
    qw, kw = n_heads * HEAD_DIM, n_kv * HEAD_DIM
    est = 2 * tm * (qw + kw) * 2 * 2 + 6 * tm * HEAD_DIM * 4
    kern = functools.partial(_qk_kernel, rope=rope, q_scale=LOG2_E * HEAD_DIM ** -0.5)
    return pl.pallas_call(
        kern,
        out_shape=(jax.ShapeDtypeStruct((b, n_heads, t, HEAD_DIM), BF16),
                   jax.ShapeDtypeStruct((b, n_kv, t, HEAD_DIM), BF16)),
        grid=(b, t // tm),
        in_specs=[
            pl.BlockSpec((1, tm, qw), lambda bi, i: (bi, i, q_blk)),
            pl.BlockSpec((1, tm, kw), lambda bi, i: (bi, i, k_blk)),
            pl.BlockSpec((1, HEAD_DIM), lambda bi, i: (0, 0)),
            pl.BlockSpec((1, HEAD_DIM), lambda bi, i: (0, 0)),
            pl.BlockSpec((tm, HEAD_DIM), lambda bi, i: (i, 0)),
            pl.BlockSpec((tm, HEAD_DIM), lambda bi, i: (i, 0)),
        ],
        out_specs=(pl.BlockSpec((1, n_heads, tm, HEAD_DIM), lambda bi, i: (bi, 0, i, 0)),
                   pl.BlockSpec((1, n_kv, tm, HEAD_DIM), lambda bi, i: (bi, 0, i, 0))),
        compiler_params=_params(("parallel", "parallel"), est),
        name="qk_norm_rope" if rope else "qk_norm",
    )(pq, pk, gq.reshape(1, HEAD_DIM), gk.reshape(1, HEAD_DIM), cc, ss)


def _attn_kernel(q_ref, kt_ref, v_ref, o_ref):
    kt = kt_ref[0, 0]
    v = jnp.concatenate([v_ref[0], jnp.ones_like(v_ref[0])], axis=1)
    for gi in range(q_ref.shape[1]):
        s = jnp.dot(q_ref[0, gi], kt, preferred_element_type=F32)
        m = jnp.max(s, axis=-1, keepdims=True)
        p = jnp.exp2(s - m).astype(BF16)
        o = jnp.dot(p, v, preferred_element_type=F32)
        o_ref[0, :, gi * HEAD_DIM:(gi + 1) * HEAD_DIM] = (
            o[:, :HEAD_DIM] / o[:, HEAD_DIM:HEAD_DIM + 1]).astype(o_ref.dtype)


def _attention(q, kt, v):
    b, h, t, _ = q.shape
    n_kv, l = kt.shape[1], kt.shape[3]
    tq = _tile(t, 1024)
    gw = GROUP * HEAD_DIM
    est = 2 * (GROUP * tq * HEAD_DIM * 2 + 3 * HEAD_DIM * l * 2 + tq * gw * 2) + GROUP * tq * l * 8
    return pl.pallas_call(
        _attn_kernel,
        out_shape=jax.ShapeDtypeStruct((b, t, h * HEAD_DIM), BF16),
        grid=(b, n_kv, t // tq),
        in_specs=[
            pl.BlockSpec((1, GROUP, tq, HEAD_DIM), lambda bi, kv, i: (bi, kv, i, 0)),
            pl.BlockSpec((1, 1, HEAD_DIM, l), lambda bi, kv, i: (bi, kv, 0, 0)),
            pl.BlockSpec((1, l, HEAD_DIM), lambda bi, kv, i: (bi, 0, kv)),
        ],
        out_specs=pl.BlockSpec((1, tq, gw), lambda bi, kv, i: (bi, i, kv)),
        compiler_params=_params(("parallel", "parallel", "parallel"), est),
        name="gqa_attention",
    )(q, kt, v)


def _aproj_kernel(a_ref, ga_ref, w_ref, o_ref):
    t = jnp.dot(a_ref[0], w_ref[...], preferred_element_type=F32)
    o_ref[0] = (_sigmoid(ga_ref[0].astype(F32)) * t).astype(o_ref.dtype)


def _aproj(attn, px, ga_blk, w_pa):
    b, t, d = attn.shape
    tm = _tile(t, 512)
    est = 2 * d * d * 2 + 2 * 3 * tm * d * 2 + 3 * tm * d * 4
    return pl.pallas_call(
        _aproj_kernel,
        out_shape=jax.ShapeDtypeStruct((b, t, d), BF16),
        grid=(b, t // tm),
        in_specs=[
            pl.BlockSpec((1, tm, d), lambda bi, i: (bi, i, 0)),
            pl.BlockSpec((1, tm, d), lambda bi, i: (bi, i, ga_blk)),
            pl.BlockSpec((d, d), lambda bi, i: (0, 0)),
        ],
        out_specs=pl.BlockSpec((1, tm, d), lambda bi, i: (bi, i, 0)),
        compiler_params=_params(("parallel", "parallel"), est),
        name="attn_proj",
    )(attn, px, w_pa)


HALO_ROWS = 16


def _bproj_kernel(cin_ref, cout_ref, cx_ref, cin_p_ref, cx_p_ref, cin_n_ref, cx_n_ref,
                  gb_ref, t1_ref, wc_ref, w_ref, o_ref):
    i, n = pl.program_id(1), pl.num_programs(1)
    tm = cin_ref.shape[1]
    z = cin_ref[0].astype(F32) * cx_ref[0].astype(F32)
    z_prev_row = cin_p_ref[0, HALO_ROWS - 1:HALO_ROWS, :].astype(F32) * cx_p_ref[0, HALO_ROWS - 1:HALO_ROWS, :].astype(F32)
    z_next_row = cin_n_ref[0, 0:1, :].astype(F32) * cx_n_ref[0, 0:1, :].astype(F32)
    z_prev_row = jnp.where(i > 0, z_prev_row, 0.0)
    z_next_row = jnp.where(i < n - 1, z_next_row, 0.0)
    row = lax.broadcasted_iota(jnp.int32, z.shape, 0)
    z_m1 = jnp.where(row == 0, z_prev_row, pltpu.roll(z, 1, 0))
    z_p1 = jnp.where(row == tm - 1, z_next_row, pltpu.roll(z, tm - 1, 0))
    conv = wc_ref[0:1, :] * z_m1 + wc_ref[1:2, :] * z + wc_ref[2:3, :] * z_p1
    cb = (cout_ref[0].astype(F32) * conv).astype(BF16)
    t = jnp.dot(cb, w_ref[...], preferred_element_type=F32)
    o_ref[0] = (t1_ref[0].astype(F32) + _sigmoid(gb_ref[0].astype(F32)) * t).astype(o_ref.dtype)


def _bproj(px, blks, t1, w_conv, w_pb):
    b, t, _ = px.shape
    d = t1.shape[2]
    tm = _tile(t, 512)
    hb = tm // HALO_ROWS
    n_h = t // HALO_ROWS
    cin_b, cout_b, cx_b, gb_b = blks

    def main(blk):
        return pl.BlockSpec((1, tm, d), lambda bi, i: (bi, i, blk))

    def prev(blk):
        return pl.BlockSpec((1, HALO_ROWS, d), lambda bi, i: (bi, jnp.maximum(i * hb - 1, 0), blk))

    def nxt(blk):
        return pl.BlockSpec((1, HALO_ROWS, d), lambda bi, i: (bi, jnp.minimum((i + 1) * hb, n_h - 1), blk))

    est = 2 * d * d * 2 + 2 * 6 * tm * d * 2 + 6 * tm * d * 4
    return pl.pallas_call(
        _bproj_kernel,
        out_shape=jax.ShapeDtypeStruct((b, t, d), BF16),
        grid=(b, t // tm),
        in_specs=[main(cin_b), main(cout_b), main(cx_b), prev(cin_b), prev(cx_b), nxt(cin_b), nxt(cx_b),
                  main(gb_b),
                  pl.BlockSpec((1, tm, d), lambda bi, i: (bi, i, 0)),
                  pl.BlockSpec((3, d), lambda bi, i: (0, 0)),
                  pl.BlockSpec((d, d), lambda bi, i: (0, 0))],
        out_specs=pl.BlockSpec((1, tm, d), lambda bi, i: (bi, i, 0)),
        compiler_params=_params(("parallel", "parallel"), est),
        name="conv_proj",
    )(px, px, px, px, px, px, px, px, t1, w_conv, w_pb)


def _oproj_kernel(m_ref, w_ref, x_ref, gt_ref, g_ref, o_ref):
    mix = jnp.dot(m_ref[0], w_ref[...], preferred_element_type=F32)
    o_ref[0] = x_ref[0] + gt_ref[0] * _rms(mix, g_ref[...])


def _oproj(m, w_o, x, gate, g_post):
    b, t, d = x.shape
    tm = _tile(t, 512)
    per_b = gate.shape[0] > 1
    gmap = (lambda bi, i: (bi, 0, 0)) if per_b else (lambda bi, i: (0, 0, 0))
    est = 2 * d * d * 2 + 2 * tm * d * (2 + 4 + 4) + 3 * tm * d * 4
    return pl.pallas_call(
        _oproj_kernel,
        out_shape=jax.ShapeDtypeStruct((b, t, d), F32),
        grid=(b, t // tm),
        in_specs=[
            pl.BlockSpec((1, tm, d), lambda bi, i: (bi, i, 0)),
            pl.BlockSpec((d, d), lambda bi, i: (0, 0)),
            pl.BlockSpec((1, tm, d), lambda bi, i: (bi, i, 0)),
            pl.BlockSpec((1, 1, d), gmap),
            pl.BlockSpec((1, d), lambda bi, i: (0, 0)),
        ],
        out_specs=pl.BlockSpec((1, tm, d), lambda bi, i: (bi, i, 0)),
        compiler_params=_params(("parallel", "parallel"), est),
        name="out_proj",
    )(m, w_o, x, gate, g_post.reshape(1, d))


def _swiglu_step(h_ref, wg_ref, wu_ref, wd_ref, acc_ref):
    h = h_ref[...]
    g = jnp.dot(h, wg_ref[...].astype(BF16), preferred_element_type=F32)
    u = jnp.dot(h, wu_ref[...].astype(BF16), preferred_element_type=F32)
    a = (g * _sigmoid(g) * u).astype(BF16)
    acc_ref[...] += jnp.dot(a, wd_ref[...].astype(BF16), preferred_element_type=F32)


def _ffn_kernel(x_ref, g1_ref, sh_ref, sc_ref, gt_ref, g2_ref, wg_ref, wu_ref, wd_ref, o_ref, h_ref, acc_ref):
    j, nj = pl.program_id(2), pl.num_programs(2)

    @pl.when(j == 0)
    def _():
        _prenorm_to(h_ref, x_ref, g1_ref, sh_ref, sc_ref)
        acc_ref[...] = jnp.zeros_like(acc_ref)

    _swiglu_step(h_ref, wg_ref, wu_ref, wd_ref, acc_ref)

    @pl.when(j == nj - 1)
    def _():
        o_ref[0] = x_ref[0] + gt_ref[0] * _rms(acc_ref[...], g2_ref[...])


def _ffn_dense(x, g_pre, shift, scale, gate, g_post, wg, wu, wd):
    b, t, d = x.shape
    f = wg.shape[1]
    tm = _tile(t, 512)
    tf = _tile(f, 1024)
    per_b = gate.shape[0] > 1
    mmap = (lambda bi, i, j: (bi, 0, 0)) if per_b else (lambda bi, i, j: (0, 0, 0))
    vec = pl.BlockSpec((1, d), lambda bi, i, j: (0, 0))
    mod = pl.BlockSpec((1, 1, d), mmap)
    est = 4 * tm * d * 4 + tm * d * 2 + tm * d * 4 + 2 * 3 * d * tf * 2 + 4 * tm * tf * 4 + 2 * tm * d * 4
    return pl.pallas_call(
        _ffn_kernel,
        out_shape=jax.ShapeDtypeStruct((b, t, d), F32),
        grid=(b, t // tm, f // tf),
        in_specs=[
            pl.BlockSpec((1, tm, d), lambda bi, i, j: (bi, i, 0)),
            vec, mod, mod, mod, vec,
            pl.BlockSpec((d, tf), lambda bi, i, j: (0, j)),
            pl.BlockSpec((d, tf), lambda bi, i, j: (0, j)),
            pl.BlockSpec((tf, d), lambda bi, i, j: (j, 0)),
        ],
        out_specs=pl.BlockSpec((1, tm, d), lambda bi, i, j: (bi, i, 0)),
        scratch_shapes=[pltpu.VMEM((tm, d), BF16), pltpu.VMEM((tm, d), F32)],
        compiler_params=_params(("parallel", "parallel", "arbitrary"), est),
        name="ffn_dense",
    )(x, g_pre.reshape(1, d), shift, scale, gate, g_post.reshape(1, d), wg, wu, wd)


HI16 = 0xFFFF0000


def _pack_rows(v):
    half = v.shape[1] // 2
    lo = pltpu.bitcast(v[:, :half].astype(BF16).astype(F32), jnp.uint32)
    hi = pltpu.bitcast(v[:, half:].astype(BF16).astype(F32), jnp.uint32)
    return (lo >> 16) | (hi & jnp.uint32(HI16))


def _unpack_rows(u):
    return pltpu.bitcast(u << 16, F32), pltpu.bitcast(u & jnp.uint32(HI16), F32)


def _split_bf16(v):
    hi = v.astype(BF16)
    return hi, (v - hi.astype(F32)).astype(BF16)


def _router_kernel(x_ref, g_ref, sh_ref, sc_ref, wr_ref, h_ref, w_ref, i_ref):
    y = _rms(x_ref[0], g_ref[...])
    h = y * (1.0 + sc_ref[0]) + sh_ref[0]
    h_ref[...] = _pack_rows(h)
    h_hi, h_lo = _split_bf16(h)
    w_hi, w_lo = _split_bf16(wr_ref[...])
    logits = jnp.dot(h_hi, w_hi, preferred_element_type=F32) + (
        jnp.dot(h_lo, w_hi, preferred_element_type=F32) + jnp.dot(h_hi, w_lo, preferred_element_type=F32))
    lane = lax.broadcasted_iota(jnp.int32, logits.shape, 1)
    lane_f = lane.astype(F32)
    neg = jnp.float32(-jnp.inf)
    big = jnp.float32(V7X_LANES)
    lg = jnp.where(lane < N_EXPERTS, logits, neg)
    v1 = jnp.max(lg, axis=-1, keepdims=True)
    i1 = jnp.min(jnp.where(lg == v1, lane_f, big), axis=-1, keepdims=True)
    lg2 = jnp.where(lane_f == i1, neg, lg)
    v2 = jnp.max(lg2, axis=-1, keepdims=True)
    i2 = jnp.min(jnp.where(lg2 == v2, lane_f, big), axis=-1, keepdims=True)
    e = jnp.exp(v2 - v1)
    den = 1.0 + e
    w_ref[...] = jnp.where(lane == 0, 1.0 / den, jnp.where(lane == 1, e / den, 0.0))
    i_ref[...] = jnp.where(lane == 0, i1, jnp.where(lane == 1, i2, 0.0)).astype(jnp.int32)


def _router(x, g_pre, shift, scale, w_router_pad):
    b, t, d = x.shape
    tm = _tile(t, 512)
    nt = t // tm
    est = 2 * tm * d * 4 * 2 + 2 * d * V7X_LANES * 4 + 4 * tm * d * 4
    vec = pl.BlockSpec((1, d), lambda bi, i: (0, 0))
    mod = pl.BlockSpec((1, 1, d), lambda bi, i: (bi, 0, 0))
    row = lambda w: pl.BlockSpec((tm, w), lambda bi, i: (bi * nt + i, 0))
    return pl.pallas_call(
        _router_kernel,
        out_shape=(jax.ShapeDtypeStruct((b * t, d // 2), jnp.uint32),
                   jax.ShapeDtypeStruct((b * t, V7X_LANES), F32),
                   jax.ShapeDtypeStruct((b * t, V7X_LANES), jnp.int32)),
        grid=(b, nt),
        in_specs=[pl.BlockSpec((1, tm, d), lambda bi, i: (bi, i, 0)), vec, mod, mod,
                  pl.BlockSpec((d, V7X_LANES), lambda bi, i: (0, 0))],
        out_specs=(row(d // 2), row(V7X_LANES), row(V7X_LANES)),
        compiler_params=_params(("parallel", "parallel"), est),
        name="moe_router",
    )(x, g_pre.reshape(1, d), shift, scale, w_router_pad)


DMA_LOOP_UNROLL = 8


def _row_copy(src, src_row, dst, dst_row, sem):
    return pltpu.make_async_copy(src.at[pl.ds(src_row, 1)], dst.at[pl.ds(dst_row, 1)], sem)


def _dispatch_kernel(p1_ref, p2_ref, h_ref, xs_in_ref, xs_ref, sem):
    del xs_in_ref
    tm = h_ref.shape[0]
    base = pl.program_id(0) * tm

    def start(r, c):
        _row_copy(h_ref, r, xs_ref, p1_ref[base + r], sem).start()
        _row_copy(h_ref, r, xs_ref, p2_ref[base + r], sem).start()
        return c

    def wait(r, c):
        _row_copy(h_ref, 0, xs_ref, 0, sem).wait()
        _row_copy(h_ref, 0, xs_ref, 0, sem).wait()
        return c

    lax.fori_loop(0, tm, start, 0, unroll=DMA_LOOP_UNROLL)
    lax.fori_loop(0, tm, wait, 0, unroll=DMA_LOOP_UNROLL)


def _dispatch(h, p1, p2, n_rows):
    n, d = h.shape
    tm = _tile(n, 512)
    xs0 = jnp.zeros((n_rows, d), h.dtype)
    return pl.pallas_call(
        _dispatch_kernel,
        out_shape=jax.ShapeDtypeStruct((n_rows, d), h.dtype),
        grid_spec=pltpu.PrefetchScalarGridSpec(
            num_scalar_prefetch=2,
            grid=(n // tm,),
            in_specs=[pl.BlockSpec((tm, d), lambda i, p1, p2: (i, 0)),
                      pl.BlockSpec(memory_space=pl.ANY)],
            out_specs=pl.BlockSpec(memory_space=pl.ANY),
            scratch_shapes=[pltpu.SemaphoreType.DMA],
        ),
        input_output_aliases={3: 0},
        compiler_params=_params(("arbitrary",), 2 * tm * d * 4),
        name="moe_dispatch",
    )(p1, p2, h, xs0)


def _moe_kernel(te_ref, tv_ref, xs_ref, wg_hbm, wu_hbm, wd_hbm, ys_ref, h_ref, acc_ref, wg_buf, wu_buf, wd_buf, sem):
    i, n = pl.program_id(0), pl.num_programs(0)
    tf = wg_buf.shape[2]
    nf = wg_hbm.shape[2] // tf
    rows = tv_ref[i]
    hm = h_ref.shape[0] // 2

    def chunk_copies(tile, c, slot):
        e = te_ref[tile]
        cols = pl.ds(pl.multiple_of(c * tf, tf), tf)
        return (pltpu.make_async_copy(wg_hbm.at[e, :, cols], wg_buf.at[slot], sem.at[0, slot]),
                pltpu.make_async_copy(wu_hbm.at[e, :, cols], wu_buf.at[slot], sem.at[1, slot]),
                pltpu.make_async_copy(wd_hbm.at[e, cols, :], wd_buf.at[slot], sem.at[2, slot]))

    def start(tile, c, slot):
        for cp in chunk_copies(tile, c, slot):
            cp.start()

    @pl.when(jnp.logical_and(i == 0, rows > 0))
    def _():
        start(0, 0, 0)

    @pl.when(rows > 0)
    def _():
        half = xs_ref.shape[1]
        lo, hi = _unpack_rows(xs_ref[...])
        h_ref[:, :half] = lo.astype(BF16)
        h_ref[:, half:] = hi.astype(BF16)
        acc_ref[...] = jnp.zeros_like(acc_ref)
        next_used = jnp.logical_and(i + 1 < n, tv_ref[jnp.minimum(i + 1, n - 1)] > 0)

        def chunk(c, slot):
            @pl.when(c + 1 < nf)
            def _():
                start(i, c + 1, 1 - slot)

            @pl.when(jnp.logical_and(c + 1 == nf, next_used))
            def _():
                start(i + 1, 0, 1 - slot)

            for cp in chunk_copies(i, c, slot):
                cp.wait()

            @pl.when(rows > hm)
            def _():
                _swiglu_step(h_ref, wg_buf.at[slot], wu_buf.at[slot], wd_buf.at[slot], acc_ref)

            @pl.when(rows <= hm)
            def _():
                _swiglu_step(h_ref.at[pl.ds(0, hm)], wg_buf.at[slot], wu_buf.at[slot], wd_buf.at[slot],
                             acc_ref.at[pl.ds(0, hm)])

        def pair(k, carry):
            chunk(2 * k, 0)
            chunk(2 * k + 1, 1)
            return carry

        lax.fori_loop(0, nf // 2, pair, 0)
        ys_ref[...] = _pack_rows(acc_ref[...])

    @pl.when(rows == 0)
    def _():
        ys_ref[...] = jnp.zeros_like(ys_ref)


def _moe_experts(xs, tile_e, tile_v, wg, wu, wd, tm):
    r, dh = xs.shape
    d = 2 * dh
    f = wg.shape[2]
    tf = _tile(f, 256)
    assert (f // tf) % 2 == 0
    est = 4 * tm * dh * 4 + tm * d * 2 + tm * d * 4 + 3 * d * tf * (2 * 4 + 2) + 4 * tm * tf * 4
    return pl.pallas_call(
        _moe_kernel,
        out_shape=jax.ShapeDtypeStruct((r, dh), jnp.uint32),
        grid_spec=pltpu.PrefetchScalarGridSpec(
            num_scalar_prefetch=2,
            grid=(r // tm,),
            in_specs=[pl.BlockSpec((tm, dh), lambda i, te, tv: (i, 0)),
                      pl.BlockSpec(memory_space=pl.ANY),
                      pl.BlockSpec(memory_space=pl.ANY),
                      pl.BlockSpec(memory_space=pl.ANY)],
            out_specs=pl.BlockSpec((tm, dh), lambda i, te, tv: (i, 0)),
            scratch_shapes=[pltpu.VMEM((tm, d), BF16), pltpu.VMEM((tm, d), F32),
                            pltpu.VMEM((2, d, tf), F32), pltpu.VMEM((2, d, tf), F32), pltpu.VMEM((2, tf, d), F32),
                            pltpu.SemaphoreType.DMA((3, 2))],
        ),
        compiler_params=_params(("arbitrary",), est),
        name="moe_experts",
    )(tile_e, tile_v, xs, wg, wu, wd)


def _combine_kernel(p1_ref, p2_ref, ys_ref, x_ref, w_ref, gt_ref, g_ref, o_ref, ybuf, sem):
    i, n = pl.program_id(0), pl.num_programs(0)
    tm = x_ref.shape[0]

    def issue(step, slot):
        base = step * tm

        def body(r, c):
            _row_copy(ys_ref, p1_ref[base + r], ybuf.at[slot, 0], r, sem.at[slot]).start()
            _row_copy(ys_ref, p2_ref[base + r], ybuf.at[slot, 1], r, sem.at[slot]).start()
            return c

        lax.fori_loop(0, tm, body, 0, unroll=DMA_LOOP_UNROLL)

    @pl.when(i == 0)
    def _():
        issue(0, 0)

    @pl.when(i + 1 < n)
    def _():
        issue(i + 1, (i + 1) % 2)

    slot = i % 2

    def wait(r, c):
        _row_copy(ys_ref, 0, ybuf.at[slot, 0], 0, sem.at[slot]).wait()
        _row_copy(ys_ref, 0, ybuf.at[slot, 1], 0, sem.at[slot]).wait()
        return c

    lax.fori_loop(0, tm, wait, 0, unroll=DMA_LOOP_UNROLL)
    w = w_ref[...]
    half = ybuf.shape[3]
    a_lo, a_hi = _unpack_rows(ybuf[slot, 0])
    b_lo, b_hi = _unpack_rows(ybuf[slot, 1])
    y_lo = w[:, 0:1] * a_lo + w[:, 1:2] * b_lo
    y_hi = w[:, 0:1] * a_hi + w[:, 1:2] * b_hi
    ms = (jnp.sum(y_lo * y_lo, axis=-1, keepdims=True) + jnp.sum(y_hi * y_hi, axis=-1, keepdims=True)) / (2 * half)
    inv = lax.rsqrt(ms + EPS)
    o_ref[:, :half] = x_ref[:, :half] + gt_ref[0, :, :half] * (y_lo * inv * g_ref[:, :half])
    o_ref[:, half:] = x_ref[:, half:] + gt_ref[0, :, half:] * (y_hi * inv * g_ref[:, half:])


def _combine(ys, p1, p2, x2d, topw, gate, g_post, tokens_per_batch):
    n, d = x2d.shape
    tm = _tile(tokens_per_batch, 512)
    nt = tokens_per_batch // tm
    est = 4 * tm * d * 4 + 4 * tm * (d // 2) * 4 + 4 * tm * d * 4
    return pl.pallas_call(
        _combine_kernel,
        out_shape=jax.ShapeDtypeStruct((n, d), F32),
        grid_spec=pltpu.PrefetchScalarGridSpec(
            num_scalar_prefetch=2,
            grid=(n // tm,),
            in_specs=[pl.BlockSpec(memory_space=pl.ANY),
                      pl.BlockSpec((tm, d), lambda i, p1, p2: (i, 0)),
                      pl.BlockSpec((tm, V7X_LANES), lambda i, p1, p2: (i, 0)),
                      pl.BlockSpec((1, 1, d), lambda i, p1, p2: (i // nt, 0, 0)),
                      pl.BlockSpec((1, d), lambda i, p1, p2: (0, 0))],
            out_specs=pl.BlockSpec((tm, d), lambda i, p1, p2: (i, 0)),
            scratch_shapes=[pltpu.VMEM((2, 2, tm, d // 2), jnp.uint32), pltpu.SemaphoreType.DMA((2,))],
        ),
        compiler_params=_params(("arbitrary",), est),
        name="moe_combine",
    )(p1, p2, ys, x2d, topw, gate, g_post.reshape(1, d))


def _route_plan(e1, e2, tm, n_tiles):
    ar = jnp.arange(N_EXPERTS, dtype=jnp.int32)
    cnt = (e1[:, None] == ar).astype(jnp.int32) + (e2[:, None] == ar).astype(jnp.int32)
    csum = jnp.cumsum(cnt, axis=0)
    padded = ((csum[-1] + tm - 1) // tm) * tm
    ends = jnp.cumsum(padded)
    pos = (ends - padded)[None, :] + csum - 1
    p1 = jnp.take_along_axis(pos, e1[:, None], axis=1)[:, 0]
    p2 = jnp.take_along_axis(pos, e2[:, None], axis=1)[:, 0]
    starts = jnp.arange(n_tiles, dtype=jnp.int32) * tm
    tile_e = jnp.sum((starts[:, None] >= ends[None, :]).astype(jnp.int32), axis=1)
    in_use = starts < ends[-1]
    last_e = jnp.max(jnp.where(in_use, tile_e, 0))
    tile_e = jnp.where(in_use, tile_e, last_e)
    group_end = (ends - padded + csum[-1])[tile_e]
    tile_rows = jnp.where(in_use, jnp.clip(group_end - starts, 0, tm), 0)
    return p1.astype(jnp.int32), p2.astype(jnp.int32), tile_e.astype(jnp.int32), tile_rows.astype(jnp.int32)


def _ffn_moe(x, g_pre, shift, scale, gate, g_post, w_router, wg, wu, wd):
    b, t, d = x.shape
    n = b * t
    wr = jnp.zeros((d, V7X_LANES), F32).at[:, :N_EXPERTS].set(w_router)
    h, topw, topi = _router(x, g_pre, shift, scale, wr)
    tm = _tile(n, 1024)
    n_tiles = (2 * n) // tm + N_EXPERTS
    p1, p2, tile_e, tile_v = _route_plan(topi[:, 0], topi[:, 1], tm, n_tiles)
    xs = _dispatch(h, p1, p2, n_tiles * tm)
    ys = _moe_experts(xs, tile_e, tile_v, wg, wu, wd, tm)
    out = _combine(ys, p1, p2, x.reshape(n, d), topw, gate, g_post, t)
    return out.reshape(b, t, d)


def _deinterleave_heads(w):
    lead, n = w.shape[:-1], w.shape[-1] // HEAD_DIM
    w = w.reshape(*lead, n, HEAD_DIM // 2, 2)
    return jnp.swapaxes(w, -1, -2).reshape(*lead, n * HEAD_DIM)


def _prep_w_in(w_in):
    d = w_in.shape[1]
    kv = d // GROUP
    w = w_in.astype(BF16)
    return w.at[..., :d + kv].set(_deinterleave_heads(w[..., :d + kv]))


def _rope_tables(t):
    rows = t // GRID_W
    row = jnp.repeat(jnp.arange(rows), GRID_W).astype(F32)
    col = jnp.tile(jnp.arange(GRID_W), rows).astype(F32)
    n_pairs_axis = HEAD_DIM // 4
    inv = ROPE_THETA ** (-jnp.arange(n_pairs_axis, dtype=F32) / n_pairs_axis)
    ang = jnp.concatenate([row[:, None] * inv[None, :], col[:, None] * inv[None, :]], axis=-1)
    cos, sin = jnp.cos(ang), jnp.sin(ang)
    return jnp.concatenate([cos, cos], axis=-1), jnp.concatenate([-sin, sin], axis=-1)


def _mixer(tok, mod, l, last_unused, w_in_l, gq, gk, w_conv_l, w_pa_l, w_pb_l, w_o_l, g_pre, g_post,
           cc, ss, rope, extra_k=None, extra_v=None):
    del l, last_unused
    b, t, d = tok.shape
    n_heads, n_kv = d // HEAD_DIM, d // HEAD_DIM // GROUP
    kv = n_kv * HEAD_DIM
    sh1, sc1, gt1 = mod
    flat, unflat = _flatteners(b, t, shared=sh1.shape[0] == 1)
    px = unflat(_inproj(flat(tok), g_pre, sh1, sc1, w_in_l, kv_after_q=2 * kv))
    q_r, k_r = _qk_norm_rope(px, 0, px, 6 * d // kv, n_heads, n_kv, gq, gk, cc, ss, rope)
    k_all, v_all = k_r, px[:, :, 6 * d + kv:]
    if extra_k is not None:
        k_all = jnp.concatenate([extra_k, k_r], axis=2)
        v_all = jnp.concatenate([extra_v, v_all], axis=1)
    attn = _attention(q_r, jnp.swapaxes(k_all, 2, 3), v_all)
    t1 = unflat(_aproj(flat(attn), flat(px), 4, w_pa_l))
    m = _bproj(px, (1, 2, 3, 5), t1, w_conv_l, w_pb_l)
    out = unflat(_oproj(flat(m), w_o_l, flat(tok), gt1, g_post))
    return out, k_r, px


def _flatteners(b, t, shared):
    if not shared:
        return (lambda a: a), (lambda a: a)
    return (lambda a: a.reshape(1, b * t, a.shape[-1])), (lambda a: a.reshape(b, t, a.shape[-1]))


def kernel(x, c, ctx, c_ctx, w_ada, b_ada, g_pre_mix, g_post_mix, g_pre_ffn, g_post_ffn, w_in, g_q, g_k,
           w_conv, w_pa, w_pb, w_o, w_dense_gate, w_dense_up, w_dense_down, w_router, w_moe_gate, w_moe_up,
           w_moe_down):
    b, t, d = x.shape
    c_len = ctx.shape[1]
    depth = w_in.shape[0]
    n_kv = d // HEAD_DIM // GROUP
    kv = n_kv * HEAD_DIM

    w_in_p = _prep_w_in(w_in)
    gq_p, gk_p = _deinterleave_heads(g_q), _deinterleave_heads(g_k)
    cc, ss = _rope_tables(t)
    cc_c, ss_c = cc[:c_len], ss[:c_len]

    n_rows = -(-(b + 1) // 16) * 16
    c_rows = jnp.zeros((n_rows, d), F32).at[:b].set(c).at[b].set(c_ctx)
    mods = _ada(c_rows, w_ada, b_ada)

    for l in range(depth):
        last = l == depth - 1
        mx = [mods[l, :b, i * d:(i + 1) * d].reshape(b, 1, d) for i in range(6)]
        mc = [mods[l, b:b + 1, i * d:(i + 1) * d].reshape(1, 1, d) for i in range(6)]
        wpa, wpb, wo = w_pa[l].astype(BF16), w_pb[l].astype(BF16), w_o[l].astype(BF16)

        flat_c, unflat_c = _flatteners(b, c_len, shared=True)
        if last:
            pc_kv = unflat_c(_inproj(flat_c(ctx), g_pre_mix[l], mc[0], mc[1], w_in_p[l][:, d:d + 2 * kv]))
            _, kc = _qk_norm_rope(pc_kv, 0, pc_kv, 0, n_kv, n_kv, gk_p[l], gk_p[l], cc_c, ss_c, False)
            vc = pc_kv[:, :, kv:]
            ctx_new = None
        else:
            ctx_new, kc, pc = _mixer(ctx, (mc[0], mc[1], mc[2]), l, None, w_in_p[l], gq_p[l], gk_p[l], w_conv[l],
                                     wpa, wpb, wo, g_pre_mix[l], g_post_mix[l], cc_c, ss_c, False)
            vc = pc[:, :, 6 * d + kv:]
        x, _, _ = _mixer(x, (mx[0], mx[1], mx[2]), l, None, w_in_p[l], gq_p[l], gk_p[l], w_conv[l],
                         wpa, wpb, wo, g_pre_mix[l], g_post_mix[l], cc, ss, True, extra_k=kc, extra_v=vc)
        if not last:
            ctx = ctx_new

        i = l // 2
        if l % 2 == 0:
            wg, wu, wd = (w_dense_gate[i].astype(BF16), w_dense_up[i].astype(BF16), w_dense_down[i].astype(BF16))
            x = _ffn_dense(x, g_pre_ffn[l], mx[3], mx[4], mx[5], g_post_ffn[l], wg, wu, wd)
            if not last:
                ctx = unflat_c(_ffn_dense(flat_c(ctx), g_pre_ffn[l], mc[3], mc[4], mc[5], g_post_ffn[l], wg, wu, wd))
        else:
            wg, wu, wd = w_moe_gate[i], w_moe_up[i], w_moe_down[i]
            x = _ffn_moe(x, g_pre_ffn[l], mx[3], mx[4], mx[5], g_post_ffn[l], w_router[i], wg, wu, wd)
            if not last:
                ctx = _ffn_moe(ctx, g_pre_ffn[l], jnp.broadcast_to(mc[3], (b, 1, d)), jnp.broadcast_to(mc[4], (b, 1, d)),
                               jnp.broadcast_to(mc[5], (b, 1, d)), g_post_ffn[l], w_router[i], wg, wu, wd)
    return x
```

```python
import functools

import jax
import jax.numpy as jnp
from jax import lax
from jax.experimental import pallas as pl
from jax.experimental.pallas import tpu as pltpu

HEAD_DIM = 128
GROUP = 4
GRID_W = 64
N_EXPERTS = 8
EPS = 1e-6
ROPE_THETA = 10000.0
LOG2_E = 1.4426950408889634

V7X_LANES = 128
V7X_VMEM_BYTES = 64 * 1024 * 1024
VMEM_RESERVE_BYTES = 8 * 1024 * 1024

F32 = jnp.float32
BF16 = jnp.bfloat16


def _params(semantics, est_bytes):
    limit = int(min(V7X_VMEM_BYTES - VMEM_RESERVE_BYTES, max(32 * 1024 * 1024, 2 * est_bytes)))
    return pltpu.CompilerParams(dimension_semantics=semantics, vmem_limit_bytes=limit)


def _tile(n, pref):
    t = min(n, pref)
    while n % t:
        t //= 2
    return t


def _rms(x32, g):
    ms = jnp.mean(x32 * x32, axis=-1, keepdims=True)
    return x32 * lax.rsqrt(ms + EPS) * g


def _sigmoid(x):
    return 1.0 / (1.0 + jnp.exp(-x))


NORM_CHUNK_ROWS = 256


def _prenorm_to(h_ref, x_ref, g_ref, sh_ref, sc_ref):
    tm = h_ref.shape[0]
    rows = min(tm, NORM_CHUNK_ROWS)

    def body(r, c):
        sl = pl.ds(pl.multiple_of(r * rows, rows), rows)
        y = _rms(x_ref[0, sl, :], g_ref[...])
        h_ref[sl, :] = (y * (1.0 + sc_ref[0]) + sh_ref[0]).astype(h_ref.dtype)
        return c

    lax.fori_loop(0, tm // rows, body, 0)


def _ada_kernel(c_ref, w_ref, b_ref, o_ref):
    c = c_ref[...]
    s = (c * _sigmoid(c)).astype(BF16)
    w = w_ref[0].astype(BF16)
    o_ref[0] = jnp.dot(s, w, preferred_element_type=F32) + b_ref[0]


def _ada(c_rows, w_ada, b_ada):
    depth, d, n = w_ada.shape
    r = c_rows.shape[0]
    tn = _tile(n, 1024)
    est = 2 * d * tn * 4 + 2 * r * tn * 4 + r * d * 4
    return pl.pallas_call(
        _ada_kernel,
        out_shape=jax.ShapeDtypeStruct((depth, r, n), F32),
        grid=(depth, n // tn),
        in_specs=[
            pl.BlockSpec((r, d), lambda l, j: (0, 0)),
            pl.BlockSpec((1, d, tn), lambda l, j: (l, 0, j)),
            pl.BlockSpec((1, 1, tn), lambda l, j: (l, 0, j)),
        ],
        out_specs=pl.BlockSpec((1, r, tn), lambda l, j: (l, 0, j)),
        compiler_params=_params(("parallel", "parallel"), est),
        name="ada_mod",
    )(c_rows, w_ada, b_ada.reshape(depth, 1, n))


def _inproj_kernel(x_ref, g_ref, sh_ref, sc_ref, w_ref, o_ref, h_ref):
    @pl.when(pl.program_id(2) == 0)
    def _():
        _prenorm_to(h_ref, x_ref, g_ref, sh_ref, sc_ref)

    o_ref[0] = jnp.dot(h_ref[...], w_ref[...], preferred_element_type=F32).astype(o_ref.dtype)


def _inproj(x, g, shift, scale, w, kv_after_q=0):
    b, t, d = x.shape
    n = w.shape[1]
    tm = _tile(t, 1024)
    tn = _tile(n, 1024)
    assert d % tn == 0 and kv_after_q % tn == 0
    nq, nk, nb = d // tn, kv_after_q // tn, n // tn

    def out_col(j):
        return jnp.where(j < nq, j, jnp.where(j < nq + nk, j + (nb - nq - nk), j - nk))

    per_b = shift.shape[0] > 1
    mod_map = (lambda bi, i, j: (bi, 0, 0)) if per_b else (lambda bi, i, j: (0, 0, 0))
    est = 2 * tm * d * 4 + tm * d * 2 + 2 * d * tn * 2 + 2 * tm * tn * 2 + 3 * tm * d * 4
    return pl.pallas_call(
        _inproj_kernel,
        out_shape=jax.ShapeDtypeStruct((b, t, n), BF16),
        grid=(b, t // tm, n // tn),
        in_specs=[
            pl.BlockSpec((1, tm, d), lambda bi, i, j: (bi, i, 0)),
            pl.BlockSpec((1, d), lambda bi, i, j: (0, 0)),
            pl.BlockSpec((1, 1, d), mod_map),
            pl.BlockSpec((1, 1, d), mod_map),
            pl.BlockSpec((d, tn), lambda bi, i, j: (0, j)),
        ],
        out_specs=pl.BlockSpec((1, tm, tn), lambda bi, i, j: (bi, i, out_col(j))),
        scratch_shapes=[pltpu.VMEM((tm, d), BF16)],
        compiler_params=_params(("parallel", "parallel", "arbitrary"), est),
        name="inproj",
    )(x, g.reshape(1, d), shift, scale, w)


def _qk_kernel(q_ref, k_ref, gq_ref, gk_ref, cc_ref, ss_ref, qo_ref, ko_ref, *, rope, q_scale):
    ones = jnp.ones((HEAD_DIM, HEAD_DIM), BF16)

    def head(t, g, scale):
        t32 = t.astype(F32)
        ss_all = jnp.dot((t32 * t32).astype(BF16), ones, preferred_element_type=F32)
        y = t32 * lax.rsqrt(ss_all * (1.0 / HEAD_DIM) + EPS) * g
        if rope:
            y = y * cc_ref[...] + pltpu.roll(y, HEAD_DIM // 2, 1) * ss_ref[...]
        if scale != 1.0:
            y = y * scale
        return y.astype(BF16)

    for hh in range(qo_ref.shape[1]):
        qo_ref[0, hh] = head(q_ref[0, :, hh * HEAD_DIM:(hh + 1) * HEAD_DIM], gq_ref[...], q_scale)
    for hh in range(ko_ref.shape[1]):
        ko_ref[0, hh] = head(k_ref[0, :, hh * HEAD_DIM:(hh + 1) * HEAD_DIM], gk_ref[...], 1.0)


def _qk_norm_rope(pq, q_blk, pk, k_blk, n_heads, n_kv, gq, gk, cc, ss, rope):
    b, t = pq.shape[:2]
    tm = _tile(t, 512)
    qw, kw = n_heads * HEAD_DIM, n_kv * HEAD_DIM
    est = 2 * tm * (qw + kw) * 2 * 2 + 6 * tm * HEAD_DIM * 4
    kern = functools.partial(_qk_kernel, rope=rope, q_scale=LOG2_E * HEAD_DIM ** -0.5)
    return pl.pallas_call(
        kern,
        out_shape=(jax.ShapeDtypeStruct((b, n_heads, t, HEAD_DIM), BF16),
                   jax.ShapeDtypeStruct((b, n_kv, t, HEAD_DIM), BF16)),
        grid=(b, t // tm),
        in_specs=[
            pl.BlockSpec((1, tm, qw), lambda bi, i: (bi, i, q_blk)),
            pl.BlockSpec((1, tm, kw), lambda bi, i: (bi, i, k_blk)),
            pl.BlockSpec((1, HEAD_DIM), lambda bi, i: (0, 0)),
            pl.BlockSpec((1, HEAD_DIM), lambda bi, i: (0, 0)),
            pl.BlockSpec((tm, HEAD_DIM), lambda bi, i: (i, 0)),
            pl.BlockSpec((tm, HEAD_DIM), lambda bi, i: (i, 0)),
        ],
        out_specs=(pl.BlockSpec((1, n_heads, tm, HEAD_DIM), lambda bi, i: (bi, 0, i, 0)),
                   pl.BlockSpec((1, n_kv, tm, HEAD_DIM), lambda bi, i: (bi, 0, i, 0))),
        compiler_params=_params(("parallel", "parallel"), est),
        name="qk_norm_rope" if rope else "qk_norm",
    )(pq, pk, gq.reshape(1, HEAD_DIM), gk.reshape(1, HEAD_DIM), cc, ss)


def _attn_kernel(q_ref, kt_ref, v_ref, o_ref):
    kt = kt_ref[0, 0]
    v = jnp.concatenate([v_ref[0], jnp.ones_like(v_ref[0])], axis=1)
    for gi in range(q_ref.shape[1]):
        s = jnp.dot(q_ref[0, gi], kt, preferred_element_type=F32)
        m = jnp.max(s, axis=-1, keepdims=True)
        p = jnp.exp2(s - m).astype(BF16)
        o = jnp.dot(p, v, preferred_element_type=F32)
        o_ref[0, :, gi * HEAD_DIM:(gi + 1) * HEAD_DIM] = (
            o[:, :HEAD_DIM] / o[:, HEAD_DIM:HEAD_DIM + 1]).astype(o_ref.dtype)


def _attention(q, kt, v):
    b, h, t, _ = q.shape
    n_kv, l = kt.shape[1], kt.shape[3]
    tq = _tile(t, 1024)
    gw = GROUP * HEAD_DIM
    est = 2 * (GROUP * tq * HEAD_DIM * 2 + 3 * HEAD_DIM * l * 2 + tq * gw * 2) + GROUP * tq * l * 8
    return pl.pallas_call(
        _attn_kernel,
        out_shape=jax.ShapeDtypeStruct((b, t, h * HEAD_DIM), BF16),
        grid=(b, n_kv, t // tq),
        in_specs=[
            pl.BlockSpec((1, GROUP, tq, HEAD_DIM), lambda bi, kv, i: (bi, kv, i, 0)),
            pl.BlockSpec((1, 1, HEAD_DIM, l), lambda bi, kv, i: (bi, kv, 0, 0)),
            pl.BlockSpec((1, l, HEAD_DIM), lambda bi, kv, i: (bi, 0, kv)),
        ],
        out_specs=pl.BlockSpec((1, tq, gw), lambda bi, kv, i: (bi, i, kv)),
        compiler_params=_params(("parallel", "parallel", "parallel"), est),
        name="gqa_attention",
    )(q, kt, v)


def _aproj_kernel(a_ref, ga_ref, w_ref, o_ref):
    t = jnp.dot(a_ref[0], w_ref[...], preferred_element_type=F32)
    o_ref[0] = (_sigmoid(ga_ref[0].astype(F32)) * t).astype(o_ref.dtype)


def _aproj(attn, px, ga_blk, w_pa):
    b, t, d = attn.shape
    tm = _tile(t, 512)
    est = 2 * d * d * 2 + 2 * 3 * tm * d * 2 + 3 * tm * d * 4
    return pl.pallas_call(
        _aproj_kernel,
        out_shape=jax.ShapeDtypeStruct((b, t, d), BF16),
        grid=(b, t // tm),
        in_specs=[
            pl.BlockSpec((1, tm, d), lambda bi, i: (bi, i, 0)),
            pl.BlockSpec((1, tm, d), lambda bi, i: (bi, i, ga_blk)),
            pl.BlockSpec((d, d), lambda bi, i: (0, 0)),
        ],
        out_specs=pl.BlockSpec((1, tm, d), lambda bi, i: (bi, i, 0)),
        compiler_params=_params(("parallel", "parallel"), est),
        name="attn_proj",
    )(attn, px, w_pa)


HALO_ROWS = 16


def _bproj_kernel(cin_ref, cout_ref, cx_ref, cin_p_ref, cx_p_ref, cin_n_ref, cx_n_ref,
                  gb_ref, t1_ref, wc_ref, w_ref, o_ref):
    i, n = pl.program_id(1), pl.num_programs(1)
    tm = cin_ref.shape[1]
    z = cin_ref[0].astype(F32) * cx_ref[0].astype(F32)
    z_prev_row = cin_p_ref[0, HALO_ROWS - 1:HALO_ROWS, :].astype(F32) * cx_p_ref[0, HALO_ROWS - 1:HALO_ROWS, :].astype(F32)
    z_next_row = cin_n_ref[0, 0:1, :].astype(F32) * cx_n_ref[0, 0:1, :].astype(F32)
    z_prev_row = jnp.where(i > 0, z_prev_row, 0.0)
    z_next_row = jnp.where(i < n - 1, z_next_row, 0.0)
    row = lax.broadcasted_iota(jnp.int32, z.shape, 0)
    z_m1 = jnp.where(row == 0, z_prev_row, pltpu.roll(z, 1, 0))
    z_p1 = jnp.where(row == tm - 1, z_next_row, pltpu.roll(z, tm - 1, 0))
    conv = wc_ref[0:1, :] * z_m1 + wc_ref[1:2, :] * z + wc_ref[2:3, :] * z_p1
    cb = (cout_ref[0].astype(F32) * conv).astype(BF16)
    t = jnp.dot(cb, w_ref[...], preferred_element_type=F32)
    o_ref[0] = (t1_ref[0].astype(F32) + _sigmoid(gb_ref[0].astype(F32)) * t).astype(o_ref.dtype)


def _bproj(px, blks, t1, w_conv, w_pb):
    b, t, _ = px.shape
    d = t1.shape[2]
    tm = _tile(t, 512)
    hb = tm // HALO_ROWS
    n_h = t // HALO_ROWS
    cin_b, cout_b, cx_b, gb_b = blks

    def main(blk):
        return pl.BlockSpec((1, tm, d), lambda bi, i: (bi, i, blk))

    def prev(blk):
        return pl.BlockSpec((1, HALO_ROWS, d), lambda bi, i: (bi, jnp.maximum(i * hb - 1, 0), blk))

    def nxt(blk):
        return pl.BlockSpec((1, HALO_ROWS, d), lambda bi, i: (bi, jnp.minimum((i + 1) * hb, n_h - 1), blk))

    est = 2 * d * d * 2 + 2 * 6 * tm * d * 2 + 6 * tm * d * 4
    return pl.pallas_call(
        _bproj_kernel,
        out_shape=jax.ShapeDtypeStruct((b, t, d), BF16),
        grid=(b, t // tm),
        in_specs=[main(cin_b), main(cout_b), main(cx_b), prev(cin_b), prev(cx_b), nxt(cin_b), nxt(cx_b),
                  main(gb_b),
                  pl.BlockSpec((1, tm, d), lambda bi, i: (bi, i, 0)),
                  pl.BlockSpec((3, d), lambda bi, i: (0, 0)),
                  pl.BlockSpec((d, d), lambda bi, i: (0, 0))],
        out_specs=pl.BlockSpec((1, tm, d), lambda bi, i: (bi, i, 0)),
        compiler_params=_params(("parallel", "parallel"), est),
        name="conv_proj",
    )(px, px, px, px, px, px, px, px, t1, w_conv, w_pb)


def _oproj_kernel(m_ref, w_ref, x_ref, gt_ref, g_ref, o_ref):
    mix = jnp.dot(m_ref[0], w_ref[...], preferred_element_type=F32)
    o_ref[0] = x_ref[0] + gt_ref[0] * _rms(mix, g_ref[...])


def _oproj(m, w_o, x, gate, g_post):
    b, t, d = x.shape
    tm = _tile(t, 512)
    per_b = gate.shape[0] > 1
    gmap = (lambda bi, i: (bi, 0, 0)) if per_b else (lambda bi, i: (0, 0, 0))
    est = 2 * d * d * 2 + 2 * tm * d * (2 + 4 + 4) + 3 * tm * d * 4
    return pl.pallas_call(
        _oproj_kernel,
        out_shape=jax.ShapeDtypeStruct((b, t, d), F32),
        grid=(b, t // tm),
        in_specs=[
            pl.BlockSpec((1, tm, d), lambda bi, i: (bi, i, 0)),
            pl.BlockSpec((d, d), lambda bi, i: (0, 0)),
            pl.BlockSpec((1, tm, d), lambda bi, i: (bi, i, 0)),
            pl.BlockSpec((1, 1, d), gmap),
            pl.BlockSpec((1, d), lambda bi, i: (0, 0)),
        ],
        out_specs=pl.BlockSpec((1, tm, d), lambda bi, i: (bi, i, 0)),
        compiler_params=_params(("parallel", "parallel"), est),
        name="out_proj",
    )(m, w_o, x, gate, g_post.reshape(1, d))


def _swiglu_step(h_ref, wg_ref, wu_ref, wd_ref, acc_ref):
    h = h_ref[...]
    g = jnp.dot(h, wg_ref[...].astype(BF16), preferred_element_type=F32)
    u = jnp.dot(h, wu_ref[...].astype(BF16), preferred_element_type=F32)
    a = (g * _sigmoid(g) * u).astype(BF16)
    acc_ref[...] += jnp.dot(a, wd_ref[...].astype(BF16), preferred_element_type=F32)


def _ffn_kernel(x_ref, g1_ref, sh_ref, sc_ref, gt_ref, g2_ref, wg_ref, wu_ref, wd_ref, o_ref, h_ref, acc_ref):
    j, nj = pl.program_id(2), pl.num_programs(2)

    @pl.when(j == 0)
    def _():
        _prenorm_to(h_ref, x_ref, g1_ref, sh_ref, sc_ref)
        acc_ref[...] = jnp.zeros_like(acc_ref)

    _swiglu_step(h_ref, wg_ref, wu_ref, wd_ref, acc_ref)

    @pl.when(j == nj - 1)
    def _():
        o_ref[0] = x_ref[0] + gt_ref[0] * _rms(acc_ref[...], g2_ref[...])


def _ffn_dense(x, g_pre, shift, scale, gate, g_post, wg, wu, wd):
    b, t, d = x.shape
    f = wg.shape[1]
    tm = _tile(t, 512)
    tf = _tile(f, 1024)
    per_b = gate.shape[0] > 1
    mmap = (lambda bi, i, j: (bi, 0, 0)) if per_b else (lambda bi, i, j: (0, 0, 0))
    vec = pl.BlockSpec((1, d), lambda bi, i, j: (0, 0))
    mod = pl.BlockSpec((1, 1, d), mmap)
    est = 4 * tm * d * 4 + tm * d * 2 + tm * d * 4 + 2 * 3 * d * tf * 2 + 4 * tm * tf * 4 + 2 * tm * d * 4
    return pl.pallas_call(
        _ffn_kernel,
        out_shape=jax.ShapeDtypeStruct((b, t, d), F32),
        grid=(b, t // tm, f // tf),
        in_specs=[
            pl.BlockSpec((1, tm, d), lambda bi, i, j: (bi, i, 0)),
            vec, mod, mod, mod, vec,
            pl.BlockSpec((d, tf), lambda bi, i, j: (0, j)),
            pl.BlockSpec((d, tf), lambda bi, i, j: (0, j)),
            pl.BlockSpec((tf, d), lambda bi, i, j: (j, 0)),
        ],
        out_specs=pl.BlockSpec((1, tm, d), lambda bi, i, j: (bi, i, 0)),
        scratch_shapes=[pltpu.VMEM((tm, d), BF16), pltpu.VMEM((tm, d), F32)],
        compiler_params=_params(("parallel", "parallel", "arbitrary"), est),
        name="ffn_dense",
    )(x, g_pre.reshape(1, d), shift, scale, gate, g_post.reshape(1, d), wg, wu, wd)


HI16 = 0xFFFF0000


def _pack_rows(v):
    half = v.shape[1] // 2
    lo = pltpu.bitcast(v[:, :half].astype(BF16).astype(F32), jnp.uint32)
    hi = pltpu.bitcast(v[:, half:].astype(BF16).astype(F32), jnp.uint32)
    return (lo >> 16) | (hi & jnp.uint32(HI16))


def _unpack_rows(u):
    return pltpu.bitcast(u << 16, F32), pltpu.bitcast(u & jnp.uint32(HI16), F32)


def _split_bf16(v):
    hi = v.astype(BF16)
    return hi, (v - hi.astype(F32)).astype(BF16)


def _router_kernel(x_ref, g_ref, sh_ref, sc_ref, wr_ref, h_ref, w_ref, i_ref):
    y = _rms(x_ref[0], g_ref[...])
    h = y * (1.0 + sc_ref[0]) + sh_ref[0]
    h_ref[...] = _pack_rows(h)
    h_hi, h_lo = _split_bf16(h)
    w_hi, w_lo = _split_bf16(wr_ref[...])
    logits = jnp.dot(h_hi, w_hi, preferred_element_type=F32) + (
        jnp.dot(h_lo, w_hi, preferred_element_type=F32) + jnp.dot(h_hi, w_lo, preferred_element_type=F32))
    lane = lax.broadcasted_iota(jnp.int32, logits.shape, 1)
    lane_f = lane.astype(F32)
    neg = jnp.float32(-jnp.inf)
    big = jnp.float32(V7X_LANES)
    lg = jnp.where(lane < N_EXPERTS, logits, neg)
    v1 = jnp.max(lg, axis=-1, keepdims=True)
    i1 = jnp.min(jnp.where(lg == v1, lane_f, big), axis=-1, keepdims=True)
    lg2 = jnp.where(lane_f == i1, neg, lg)
    v2 = jnp.max(lg2, axis=-1, keepdims=True)
    i2 = jnp.min(jnp.where(lg2 == v2, lane_f, big), axis=-1, keepdims=True)
    e = jnp.exp(v2 - v1)
    den = 1.0 + e
    w_ref[...] = jnp.where(lane == 0, 1.0 / den, jnp.where(lane == 1, e / den, 0.0))
    i_ref[...] = jnp.where(lane == 0, i1, jnp.where(lane == 1, i2, 0.0)).astype(jnp.int32)


def _router(x, g_pre, shift, scale, w_router_pad):
    b, t, d = x.shape
    tm = _tile(t, 512)
    nt = t // tm
    est = 2 * tm * d * 4 * 2 + 2 * d * V7X_LANES * 4 + 4 * tm * d * 4
    vec = pl.BlockSpec((1, d), lambda bi, i: (0, 0))
    mod = pl.BlockSpec((1, 1, d), lambda bi, i: (bi, 0, 0))
    row = lambda w: pl.BlockSpec((tm, w), lambda bi, i: (bi * nt + i, 0))
    return pl.pallas_call(
        _router_kernel,
        out_shape=(jax.ShapeDtypeStruct((b * t, d // 2), jnp.uint32),
                   jax.ShapeDtypeStruct((b * t, V7X_LANES), F32),
                   jax.ShapeDtypeStruct((b * t, V7X_LANES), jnp.int32)),
        grid=(b, nt),
        in_specs=[pl.BlockSpec((1, tm, d), lambda bi, i: (bi, i, 0)), vec, mod, mod,
                  pl.BlockSpec((d, V7X_LANES), lambda bi, i: (0, 0))],
        out_specs=(row(d // 2), row(V7X_LANES), row(V7X_LANES)),
        compiler_params=_params(("parallel", "parallel"), est),
        name="moe_router",
    )(x, g_pre.reshape(1, d), shift, scale, w_router_pad)


DMA_LOOP_UNROLL = 8


def _row_copy(src, src_row, dst, dst_row, sem):
    return pltpu.make_async_copy(src.at[pl.ds(src_row, 1)], dst.at[pl.ds(dst_row, 1)], sem)


def _dispatch_kernel(p1_ref, p2_ref, h_ref, xs_in_ref, xs_ref, sem):
    del xs_in_ref
    tm = h_ref.shape[0]
    base = pl.program_id(0) * tm

    def start(r, c):
        _row_copy(h_ref, r, xs_ref, p1_ref[base + r], sem).start(priority=0)
        _row_copy(h_ref, r, xs_ref, p2_ref[base + r], sem).start(priority=1)
        return c

    def wait(r, c):
        _row_copy(h_ref, 0, xs_ref, 0, sem).wait()
        _row_copy(h_ref, 0, xs_ref, 0, sem).wait()
        return c

    lax.fori_loop(0, tm, start, 0, unroll=DMA_LOOP_UNROLL)
    lax.fori_loop(0, tm, wait, 0, unroll=DMA_LOOP_UNROLL)


def _dispatch(h, p1, p2, n_rows):
    n, d = h.shape
    tm = _tile(n, 512)
    xs0 = jnp.zeros((n_rows, d), h.dtype)
    return pl.pallas_call(
        _dispatch_kernel,
        out_shape=jax.ShapeDtypeStruct((n_rows, d), h.dtype),
        grid_spec=pltpu.PrefetchScalarGridSpec(
            num_scalar_prefetch=2,
            grid=(n // tm,),
            in_specs=[pl.BlockSpec((tm, d), lambda i, p1, p2: (i, 0)),
                      pl.BlockSpec(memory_space=pl.ANY)],
            out_specs=pl.BlockSpec(memory_space=pl.ANY),
            scratch_shapes=[pltpu.SemaphoreType.DMA],
        ),
        input_output_aliases={3: 0},
        compiler_params=_params(("arbitrary",), 2 * tm * d * 4),
        name="moe_dispatch",
    )(p1, p2, h, xs0)


def _moe_kernel(te_ref, tv_ref, xs_ref, wg_hbm, wu_hbm, wd_hbm, ys_ref, h_ref, acc_ref, wg_buf, wu_buf, wd_buf, sem):
    i, n = pl.program_id(0), pl.num_programs(0)
    tf = wg_buf.shape[2]
    nf = wg_hbm.shape[2] // tf
    rows = tv_ref[i]
    hm = h_ref.shape[0] // 2

    def chunk_copies(tile, c, slot):
        e = te_ref[tile]
        cols = pl.ds(pl.multiple_of(c * tf, tf), tf)
        return (pltpu.make_async_copy(wg_hbm.at[e, :, cols], wg_buf.at[slot], sem.at[0, slot]),
                pltpu.make_async_copy(wu_hbm.at[e, :, cols], wu_buf.at[slot], sem.at[1, slot]),
                pltpu.make_async_copy(wd_hbm.at[e, cols, :], wd_buf.at[slot], sem.at[2, slot]))

    def start(tile, c, slot):
        for cp in chunk_copies(tile, c, slot):
            cp.start()

    @pl.when(jnp.logical_and(i == 0, rows > 0))
    def _():
        start(0, 0, 0)

    @pl.when(rows > 0)
    def _():
        half = xs_ref.shape[1]
        lo, hi = _unpack_rows(xs_ref[...])
        h_ref[:, :half] = lo.astype(BF16)
        h_ref[:, half:] = hi.astype(BF16)
        acc_ref[...] = jnp.zeros_like(acc_ref)
        next_used = jnp.logical_and(i + 1 < n, tv_ref[jnp.minimum(i + 1, n - 1)] > 0)

        def chunk(c, slot):
            @pl.when(c + 1 < nf)
            def _():
                start(i, c + 1, 1 - slot)

            @pl.when(jnp.logical_and(c + 1 == nf, next_used))
            def _():
                start(i + 1, 0, 1 - slot)

            for cp in chunk_copies(i, c, slot):
                cp.wait()

            @pl.when(rows > hm)
            def _():
                _swiglu_step(h_ref, wg_buf.at[slot], wu_buf.at[slot], wd_buf.at[slot], acc_ref)

            @pl.when(rows <= hm)
            def _():
                _swiglu_step(h_ref.at[pl.ds(0, hm)], wg_buf.at[slot], wu_buf.at[slot], wd_buf.at[slot],
                             acc_ref.at[pl.ds(0, hm)])

        def pair(k, carry):
            chunk(2 * k, 0)
            chunk(2 * k + 1, 1)
            return carry

        lax.fori_loop(0, nf // 2, pair, 0)
        ys_ref[...] = _pack_rows(acc_ref[...])

    @pl.when(rows == 0)
    def _():
        ys_ref[...] = jnp.zeros_like(ys_ref)


def _moe_experts(xs, tile_e, tile_v, wg, wu, wd, tm):
    r, dh = xs.shape
    d = 2 * dh
    f = wg.shape[2]
    tf = _tile(f, 256)
    assert (f // tf) % 2 == 0
    est = 4 * tm * dh * 4 + tm * d * 2 + tm * d * 4 + 3 * d * tf * (2 * 4 + 2) + 4 * tm * tf * 4
    return pl.pallas_call(
        _moe_kernel,
        out_shape=jax.ShapeDtypeStruct((r, dh), jnp.uint32),
        grid_spec=pltpu.PrefetchScalarGridSpec(
            num_scalar_prefetch=2,
            grid=(r // tm,),
            in_specs=[pl.BlockSpec((tm, dh), lambda i, te, tv: (i, 0)),
                      pl.BlockSpec(memory_space=pl.ANY),
                      pl.BlockSpec(memory_space=pl.ANY),
                      pl.BlockSpec(memory_space=pl.ANY)],
            out_specs=pl.BlockSpec((tm, dh), lambda i, te, tv: (i, 0)),
            scratch_shapes=[pltpu.VMEM((tm, d), BF16), pltpu.VMEM((tm, d), F32),
                            pltpu.VMEM((2, d, tf), F32), pltpu.VMEM((2, d, tf), F32), pltpu.VMEM((2, tf, d), F32),
                            pltpu.SemaphoreType.DMA((3, 2))],
        ),
        compiler_params=_params(("arbitrary",), est),
        name="moe_experts",
    )(tile_e, tile_v, xs, wg, wu, wd)


def _combine_kernel(p1_ref, p2_ref, ys_ref, x_ref, w_ref, gt_ref, g_ref, o_ref, ybuf, sem):
    i, n = pl.program_id(0), pl.num_programs(0)
    tm = x_ref.shape[0]

    def issue(step, slot):
        base = step * tm

        def body(r, c):
            _row_copy(ys_ref, p1_ref[base + r], ybuf.at[slot, 0], r, sem.at[slot]).start(priority=0)
            _row_copy(ys_ref, p2_ref[base + r], ybuf.at[slot, 1], r, sem.at[slot]).start(priority=1)
            return c

        lax.fori_loop(0, tm, body, 0, unroll=DMA_LOOP_UNROLL)

    @pl.when(i == 0)
    def _():
        issue(0, 0)

    @pl.when(i + 1 < n)
    def _():
        issue(i + 1, (i + 1) % 2)

    slot = i % 2

    def wait(r, c):
        _row_copy(ys_ref, 0, ybuf.at[slot, 0], 0, sem.at[slot]).wait()
        _row_copy(ys_ref, 0, ybuf.at[slot, 1], 0, sem.at[slot]).wait()
        return c

    lax.fori_loop(0, tm, wait, 0, unroll=DMA_LOOP_UNROLL)
    w = w_ref[...]
    half = ybuf.shape[3]
    a_lo, a_hi = _unpack_rows(ybuf[slot, 0])
    b_lo, b_hi = _unpack_rows(ybuf[slot, 1])
    y_lo = w[:, 0:1] * a_lo + w[:, 1:2] * b_lo
    y_hi = w[:, 0:1] * a_hi + w[:, 1:2] * b_hi
    ms = (jnp.sum(y_lo * y_lo, axis=-1, keepdims=True) + jnp.sum(y_hi * y_hi, axis=-1, keepdims=True)) / (2 * half)
    inv = lax.rsqrt(ms + EPS)
    o_ref[:, :half] = x_ref[:, :half] + gt_ref[0, :, :half] * (y_lo * inv * g_ref[:, :half])
    o_ref[:, half:] = x_ref[:, half:] + gt_ref[0, :, half:] * (y_hi * inv * g_ref[:, half:])


def _combine(ys, p1, p2, x2d, topw, gate, g_post, tokens_per_batch):
    n, d = x2d.shape
    tm = _tile(tokens_per_batch, 512)
    nt = tokens_per_batch // tm
    est = 4 * tm * d * 4 + 4 * tm * (d // 2) * 4 + 4 * tm * d * 4
    return pl.pallas_call(
        _combine_kernel,
        out_shape=jax.ShapeDtypeStruct((n, d), F32),
        grid_spec=pltpu.PrefetchScalarGridSpec(
            num_scalar_prefetch=2,
            grid=(n // tm,),
            in_specs=[pl.BlockSpec(memory_space=pl.ANY),
                      pl.BlockSpec((tm, d), lambda i, p1, p2: (i, 0)),
                      pl.BlockSpec((tm, V7X_LANES), lambda i, p1, p2: (i, 0)),
                      pl.BlockSpec((1, 1, d), lambda i, p1, p2: (i // nt, 0, 0)),
                      pl.BlockSpec((1, d), lambda i, p1, p2: (0, 0))],
            out_specs=pl.BlockSpec((tm, d), lambda i, p1, p2: (i, 0)),
            scratch_shapes=[pltpu.VMEM((2, 2, tm, d // 2), jnp.uint32), pltpu.SemaphoreType.DMA((2,))],
        ),
        compiler_params=_params(("arbitrary",), est),
        name="moe_combine",
    )(p1, p2, ys, x2d, topw, gate, g_post.reshape(1, d))


def _route_plan(e1, e2, tm, n_tiles):
    ar = jnp.arange(N_EXPERTS, dtype=jnp.int32)
    cnt = (e1[:, None] == ar).astype(jnp.int32) + (e2[:, None] == ar).astype(jnp.int32)
    csum = jnp.cumsum(cnt, axis=0)
    padded = ((csum[-1] + tm - 1) // tm) * tm
    ends = jnp.cumsum(padded)
    pos = (ends - padded)[None, :] + csum - 1
    p1 = jnp.take_along_axis(pos, e1[:, None], axis=1)[:, 0]
    p2 = jnp.take_along_axis(pos, e2[:, None], axis=1)[:, 0]
    starts = jnp.arange(n_tiles, dtype=jnp.int32) * tm
    tile_e = jnp.sum((starts[:, None] >= ends[None, :]).astype(jnp.int32), axis=1)
    in_use = starts < ends[-1]
    last_e = jnp.max(jnp.where(in_use, tile_e, 0))
    tile_e = jnp.where(in_use, tile_e, last_e)
    group_end = (ends - padded + csum[-1])[tile_e]
    tile_rows = jnp.where(in_use, jnp.clip(group_end - starts, 0, tm), 0)
    return p1.astype(jnp.int32), p2.astype(jnp.int32), tile_e.astype(jnp.int32), tile_rows.astype(jnp.int32)


def _ffn_moe(x, g_pre, shift, scale, gate, g_post, w_router, wg, wu, wd):
    b, t, d = x.shape
    n = b * t
    wr = jnp.zeros((d, V7X_LANES), F32).at[:, :N_EXPERTS].set(w_router)
    h, topw, topi = _router(x, g_pre, shift, scale, wr)
    tm = _tile(n, 1024)
    n_tiles = (2 * n) // tm + N_EXPERTS
    p1, p2, tile_e, tile_v = _route_plan(topi[:, 0], topi[:, 1], tm, n_tiles)
    xs = _dispatch(h, p1, p2, n_tiles * tm)
    ys = _moe_experts(xs, tile_e, tile_v, wg, wu, wd, tm)
    out = _combine(ys, p1, p2, x.reshape(n, d), topw, gate, g_post, t)
    return out.reshape(b, t, d)


def _deinterleave_heads(w):
    lead, n = w.shape[:-1], w.shape[-1] // HEAD_DIM
    w = w.reshape(*lead, n, HEAD_DIM // 2, 2)
    return jnp.swapaxes(w, -1, -2).reshape(*lead, n * HEAD_DIM)


def _prep_w_in(w_in):
    d = w_in.shape[1]
    kv = d // GROUP
    w = w_in.astype(BF16)
    return w.at[..., :d + kv].set(_deinterleave_heads(w[..., :d + kv]))


def _rope_tables(t):
    rows = t // GRID_W
    row = jnp.repeat(jnp.arange(rows), GRID_W).astype(F32)
    col = jnp.tile(jnp.arange(GRID_W), rows).astype(F32)
    n_pairs_axis = HEAD_DIM // 4
    inv = ROPE_THETA ** (-jnp.arange(n_pairs_axis, dtype=F32) / n_pairs_axis)
    ang = jnp.concatenate([row[:, None] * inv[None, :], col[:, None] * inv[None, :]], axis=-1)
    cos, sin = jnp.cos(ang), jnp.sin(ang)
    return jnp.concatenate([cos, cos], axis=-1), jnp.concatenate([-sin, sin], axis=-1)


def _mixer(tok, mod, l, last_unused, w_in_l, gq, gk, w_conv_l, w_pa_l, w_pb_l, w_o_l, g_pre, g_post,
           cc, ss, rope, extra_k=None, extra_v=None):
    del l, last_unused
    b, t, d = tok.shape
    n_heads, n_kv = d // HEAD_DIM, d // HEAD_DIM // GROUP
    kv = n_kv * HEAD_DIM
    sh1, sc1, gt1 = mod
    flat, unflat = _flatteners(b, t, shared=sh1.shape[0] == 1)
    px = unflat(_inproj(flat(tok), g_pre, sh1, sc1, w_in_l, kv_after_q=2 * kv))
    q_r, k_r = _qk_norm_rope(px, 0, px, 6 * d // kv, n_heads, n_kv, gq, gk, cc, ss, rope)
    k_all, v_all = k_r, px[:, :, 6 * d + kv:]
    if extra_k is not None:
        k_all = jnp.concatenate([extra_k, k_r], axis=2)
        v_all = jnp.concatenate([extra_v, v_all], axis=1)
    attn = _attention(q_r, jnp.swapaxes(k_all, 2, 3), v_all)
    t1 = unflat(_aproj(flat(attn), flat(px), 4, w_pa_l))
    m = _bproj(px, (1, 2, 3, 5), t1, w_conv_l, w_pb_l)
    out = unflat(_oproj(flat(m), w_o_l, flat(tok), gt1, g_post))
    return out, k_r, px


def _flatteners(b, t, shared):
    if not shared:
        return (lambda a: a), (lambda a: a)
    return (lambda a: a.reshape(1, b * t, a.shape[-1])), (lambda a: a.reshape(b, t, a.shape[-1]))


def kernel(x, c, ctx, c_ctx, w_ada, b_ada, g_pre_mix, g_post_mix, g_pre_ffn, g_post_ffn, w_in, g_q, g_k,
           w_conv, w_pa, w_pb, w_o, w_dense_gate, w_dense_up, w_dense_down, w_router, w_moe_gate, w_moe_up,
           w_moe_down):
    b, t, d = x.shape
    c_len = ctx.shape[1]
    depth = w_in.shape[0]
    n_kv = d // HEAD_DIM // GROUP
    kv = n_kv * HEAD_DIM

    w_in_p = _prep_w_in(w_in)
    gq_p, gk_p = _deinterleave_heads(g_q), _deinterleave_heads(g_k)
    cc, ss = _rope_tables(t)
    cc_c, ss_c = cc[:c_len], ss[:c_len]

    n_rows = -(-(b + 1) // 16) * 16
    c_rows = jnp.zeros((n_rows, d), F32).at[:b].set(c).at[b].set(c_ctx)
    mods = _ada(c_rows, w_ada, b_ada)

    for l in range(depth):
        last = l == depth - 1
        mx = [mods[l, :b, i * d:(i + 1) * d].reshape(b, 1, d) for i in range(6)]
        mc = [mods[l, b:b + 1, i * d:(i + 1) * d].reshape(1, 1, d) for i in range(6)]
        wpa, wpb, wo = w_pa[l].astype(BF16), w_pb[l].astype(BF16), w_o[l].astype(BF16)

        flat_c, unflat_c = _flatteners(b, c_len, shared=True)
        if last:
            pc_kv = unflat_c(_inproj(flat_c(ctx), g_pre_mix[l], mc[0], mc[1], w_in_p[l][:, d:d + 2 * kv]))
            _, kc = _qk_norm_rope(pc_kv, 0, pc_kv, 0, n_kv, n_kv, gk_p[l], gk_p[l], cc_c, ss_c, False)
            vc = pc_kv[:, :, kv:]
            ctx_new = None
        else:
            ctx_new, kc, pc = _mixer(ctx, (mc[0], mc[1], mc[2]), l, None, w_in_p[l], gq_p[l], gk_p[l], w_conv[l],
                                     wpa, wpb, wo, g_pre_mix[l], g_post_mix[l], cc_c, ss_c, False)
            vc = pc[:, :, 6 * d + kv:]
        x, _, _ = _mixer(x, (mx[0], mx[1], mx[2]), l, None, w_in_p[l], gq_p[l], gk_p[l], w_conv[l],
                         wpa, wpb, wo, g_pre_mix[l], g_post_mix[l], cc, ss, True, extra_k=kc, extra_v=vc)
        if not last:
            ctx = ctx_new

        i = l // 2
        if l % 2 == 0:
            wg, wu, wd = (w_dense_gate[i].astype(BF16), w_dense_up[i].astype(BF16), w_dense_down[i].astype(BF16))
            x = _ffn_dense(x, g_pre_ffn[l], mx[3], mx[4], mx[5], g_post_ffn[l], wg, wu, wd)
            if not last:
                ctx = unflat_c(_ffn_dense(flat_c(ctx), g_pre_ffn[l], mc[3], mc[4], mc[5], g_post_ffn[l], wg, wu, wd))
        else:
            wg, wu, wd = w_moe_gate[i], w_moe_up[i], w_moe_down[i]
            x = _ffn_moe(x, g_pre_ffn[l], mx[3], mx[4], mx[5], g_post_ffn[l], w_router[i], wg, wu, wd)
            if not last:
                ctx = _ffn_moe(ctx, g_pre_ffn[l], jnp.broadcast_to(mc[3], (b, 1, d)), jnp.broadcast_to(mc[4], (b, 1, d)),
                               jnp.broadcast_to(mc[5], (b, 1, d)), g_post_ffn[l], w_router[i], wg, wu, wd)
    return x
```
